```python
import jax, jax.numpy as jnp
from jax import lax
import numpy as np

D_MODEL = 1024
BATCH = 4
SEQ = 4096
DEPTH = 1
DEC_BATCH = 32
DEC_SEQ = 8
PAST_LEN = 8192
PAGE_SIZE = 128

CHUNK = 128
G_GROUPS = 4
G_HD = 128
G_WIDTH = G_GROUPS * G_HD
SB_HEADS = 8
SB_HD = 128
SB_WIDTH = SB_HEADS * SB_HD
SB_QBLOCK = 128
SB_SCALE = SB_HD ** -0.5
SB_BIAS_HI = -5.0
SB_BIAS_LO = -9.0
MEM_HEADS = 4
MEM_HD = 128
MEM_WIDTH = MEM_HEADS * MEM_HD
N_MEM = 256
N_BRANCH = 3
IN_WIDTH = 2 * G_WIDTH + 3 * SB_WIDTH + MEM_WIDTH + N_BRANCH * D_MODEL
PEER_HEADS = 8
PEER_DQ = 256
PEER_DH = PEER_DQ // 2
N_KEYS = 128
N_EXPERTS = N_KEYS * N_KEYS
PEER_TOPK = 16
PEER_BLOCK = 128
LN_EPS = 1e-5
DEEPNORM_ALPHA = (2 * DEPTH) ** 0.25
DEEPNORM_BETA = (8 * DEPTH) ** -0.25

kernel_name = "hybrid_gmlp_stickbreak_mem_peer_step"


def layer_norm(x, g, b):
    xf = x.astype(jnp.float32)
    mu = jnp.mean(xf, axis=-1, keepdims=True)
    var = jnp.mean(jnp.square(xf - mu), axis=-1, keepdims=True)
    y = (xf - mu) * lax.rsqrt(var + LN_EPS) * g.astype(jnp.float32) + b.astype(jnp.float32)
    return y.astype(x.dtype)


def in_proj(x, w_in, b_gate, gln_g, gln_b):
    B, T = x.shape[0], x.shape[1]
    h = jnp.einsum('btd,de->bte', x, w_in)
    offs = np.cumsum((G_WIDTH, G_WIDTH, SB_WIDTH, SB_WIDTH, SB_WIDTH, MEM_WIDTH)).tolist()
    hu, hv, hq, hk, hvv, hqm, hg = jnp.split(h, offs, axis=-1)
    u = jax.nn.gelu(hu)
    v = layer_norm(jax.nn.gelu(hv), gln_g, gln_b).reshape(B, T, G_GROUPS, G_HD)
    q = hq.reshape(B, T, SB_HEADS, SB_HD)
    k = hk.reshape(B, T, SB_HEADS, SB_HD)
    vv = hvv.reshape(B, T, SB_HEADS, SB_HD)
    qm = hqm.reshape(B, T, MEM_HEADS, MEM_HD)
    gates = jax.nn.sigmoid(hg.reshape(B, T, N_BRANCH, D_MODEL) + b_gate)
    return u, v, q, k, vv, qm, gates


def spatial_gate(u, v, w_s, b_s):
    B, T = v.shape[0], v.shape[1]
    pad = (-T) % CHUNK
    vp = jnp.pad(v, ((0, 0), (0, pad), (0, 0), (0, 0)))
    nc = (T + pad) // CHUNK
    vc = vp.reshape(B, nc, CHUNK, G_GROUPS, G_HD)
    w = jnp.where(jnp.tril(jnp.ones((CHUNK, CHUNK), dtype=bool)), w_s, 0)
    s = jnp.einsum('gij,bcjgd->bcigd', w, vc) + b_s.T[:, :, None]
    s = s.reshape(B, nc * CHUNK, G_WIDTH)[:, :T]
    return u * s


def sb_attend(q, k, v, q_pos, k_pos, bias):
    z = (jnp.einsum('bqhd,bkhd->bhqk', q, k).astype(jnp.float32) * SB_SCALE
         + bias.astype(jnp.float32)[None, :, None, None])
    causal = k_pos[None, :] < q_pos[:, None]
    log_fail = jnp.where(causal, jax.nn.log_sigmoid(-z), 0.0)
    log_a = z + lax.cumsum(log_fail, axis=3, reverse=True)
    a = jnp.exp(jnp.where(causal, log_a, -jnp.inf))
    return jnp.einsum('bhqk,bkhd->bqhd', a.astype(v.dtype), v)


def sb_prompt(q, k, v, bias):
    B, S = q.shape[0], q.shape[1]
    nb = S // SB_QBLOCK
    qb = q.reshape(B, nb, SB_QBLOCK, SB_HEADS, SB_HD).swapaxes(0, 1)
    qpos = jnp.arange(S, dtype=jnp.int32).reshape(nb, SB_QBLOCK)
    kpos = jnp.arange(S, dtype=jnp.int32)
    ob = lax.map(lambda a: sb_attend(a[0], k, v, a[1], kpos, bias), (qb, qpos))
    return ob.swapaxes(0, 1).reshape(B, S, SB_HEADS, SB_HD)


def sb_sample(q, k_new, v_new, k_past, v_past, bias):
    past, T = k_past.shape[1], q.shape[1]
    k_all = jnp.concatenate([k_past, k_new], axis=1)
    v_all = jnp.concatenate([v_past, v_new], axis=1)
    q_pos = past + jnp.arange(T, dtype=jnp.int32)
    k_pos = jnp.arange(past + T, dtype=jnp.int32)
    return sb_attend(q, k_all, v_all, q_pos, k_pos, bias)


def mem_kv(mem, w_mem_kv):
    B, M = mem.shape[0], mem.shape[1]
    mk, mv = jnp.split(jnp.einsum('bmd,de->bme', mem, w_mem_kv), 2, axis=-1)
    return mk.reshape(B, M, MEM_HEADS, MEM_HD), mv.reshape(B, M, MEM_HEADS, MEM_HD)


def mem_attend(qm, mk, mv):
    s = jnp.einsum('bthd,bmhd->bhtm', qm, mk).astype(jnp.float32) * (MEM_HD ** -0.5)
    p = jax.nn.softmax(s, axis=-1)
    return jnp.einsum('bhtm,bmhd->bthd', p.astype(mv.dtype), mv)


def merge(gates, g_out, sb_out, m_out, w_br_g, w_br_sb, w_br_m, w_out):
    B, T = g_out.shape[0], g_out.shape[1]
    z = (gates[:, :, 0] * (g_out @ w_br_g)
         + gates[:, :, 1] * (sb_out.reshape(B, T, SB_WIDTH) @ w_br_sb)
         + gates[:, :, 2] * (m_out.reshape(B, T, MEM_WIDTH) @ w_br_m))
    return z @ w_out


def peer(x, wq, subkeys, u_tab, v_tab):
    B, T, D = x.shape
    n = B * T
    pad = (-n) % PEER_BLOCK
    xt = jnp.pad(x.reshape(n, D), ((0, pad), (0, 0))).reshape(-1, PEER_BLOCK, D)
    ncand = PEER_TOPK * PEER_TOPK

    def block(xb):
        q = jnp.einsum('td,de->te', xb, wq).reshape(PEER_BLOCK, PEER_HEADS, 2, PEER_DH)
        s = jnp.einsum('thpc,hpnc->thpn', q, subkeys).astype(jnp.float32)
        s1, i1 = lax.top_k(s[:, :, 0], PEER_TOPK)
        s2, i2 = lax.top_k(s[:, :, 1], PEER_TOPK)
        cand = (s1[..., :, None] + s2[..., None, :]).reshape(PEER_BLOCK, PEER_HEADS, ncand)
        cidx = (i1[..., :, None] * N_KEYS + i2[..., None, :]).reshape(PEER_BLOCK, PEER_HEADS, ncand)
        top, sel = lax.top_k(cand, PEER_TOPK)
        idx = jnp.take_along_axis(cidx, sel, axis=-1)
        g = jax.nn.softmax(top, axis=-1)
        hid = jnp.einsum('thkd,td->thk', u_tab[idx], xb).astype(jnp.float32)
        act = (g * jax.nn.gelu(hid)).astype(xb.dtype)
        return jnp.einsum('thk,thkd->td', act, v_tab[idx])

    y = lax.map(block, xt).reshape(-1, D)[:n]
    return y.reshape(B, T, D)


def setup_inputs(seed: int = 0) -> dict:
    key = jax.random.key(seed)
    ks = jax.random.split(key, 32)

    def nrm(k, shape, scale):
        return jax.random.normal(k, shape, jnp.float32) * scale

    n_pages = PAST_LEN // PAGE_SIZE
    n_used = DEC_BATCH * n_pages
    n_phys = n_used + n_used // 4
    page_table = jax.random.permutation(ks[5], n_phys)[:n_used].reshape(DEC_BATCH, n_pages).astype(jnp.int32)
    L = DEPTH
    sb_bias0 = jnp.linspace(SB_BIAS_HI, SB_BIAS_LO, SB_HEADS, dtype=jnp.float32)[None, :]
    return {
        "x_prompt": nrm(ks[0], (BATCH, SEQ, D_MODEL), 1.0),
        "x_sample": nrm(ks[1], (DEC_BATCH, DEC_SEQ, D_MODEL), 1.0),
        "mem_prompt": nrm(ks[2], (BATCH, N_MEM, D_MODEL), 1.0),
        "cache_sb_k": nrm(ks[3], (L, n_phys, PAGE_SIZE, SB_HEADS, SB_HD), 1.0),
        "cache_sb_v": nrm(ks[4], (L, n_phys, PAGE_SIZE, SB_HEADS, SB_HD), 1.0),
        "page_table": page_table,
        "cache_mem_k": nrm(ks[6], (L, DEC_BATCH, N_MEM, MEM_HEADS, MEM_HD), 1.0),
        "cache_mem_v": nrm(ks[7], (L, DEC_BATCH, N_MEM, MEM_HEADS, MEM_HD), 1.0),
        "w_in": nrm(ks[8], (L, D_MODEL, IN_WIDTH), D_MODEL ** -0.5),
        "b_gate": nrm(ks[9], (L, N_BRANCH, D_MODEL), 0.01),
        "gmlp_ln_g": 1.0 + nrm(ks[10], (L, G_WIDTH), 0.01),
        "gmlp_ln_b": nrm(ks[11], (L, G_WIDTH), 0.01),
        "w_spatial": nrm(ks[12], (L, G_GROUPS, CHUNK, CHUNK), CHUNK ** -0.5),
        "b_spatial": 1.0 + nrm(ks[13], (L, G_GROUPS, CHUNK), 0.1),
        "sb_bias": sb_bias0 + nrm(ks[27], (L, SB_HEADS), 0.1),
        "w_mem_kv": nrm(ks[14], (L, D_MODEL, 2 * MEM_WIDTH), D_MODEL ** -0.5),
        "w_br_gmlp": nrm(ks[15], (L, G_WIDTH, D_MODEL), DEEPNORM_BETA * G_WIDTH ** -0.5),
        "w_br_sb": nrm(ks[16], (L, SB_WIDTH, D_MODEL), DEEPNORM_BETA * SB_WIDTH ** -0.5),
        "w_br_mem": nrm(ks[17], (L, MEM_WIDTH, D_MODEL), DEEPNORM_BETA * MEM_WIDTH ** -0.5),
        "w_out": nrm(ks[18], (L, D_MODEL, D_MODEL), DEEPNORM_BETA * D_MODEL ** -0.5),
        "ln1_g": 1.0 + nrm(ks[19], (L, D_MODEL), 0.01),
        "ln1_b": nrm(ks[20], (L, D_MODEL), 0.01),
        "peer_wq": nrm(ks[21], (L, D_MODEL, PEER_HEADS * PEER_DQ), D_MODEL ** -0.5),
        "peer_subkeys": nrm(ks[22], (L, PEER_HEADS, 2, N_KEYS, PEER_DH), PEER_DH ** -0.5),
        "peer_u": nrm(ks[23], (L, N_EXPERTS, D_MODEL), D_MODEL ** -0.5),
        "peer_v": nrm(ks[24], (L, N_EXPERTS, D_MODEL), DEEPNORM_BETA * PEER_HEADS ** -0.5),
        "ln2_g": 1.0 + nrm(ks[25], (L, D_MODEL), 0.01),
        "ln2_b": nrm(ks[26], (L, D_MODEL), 0.01),
    }


def reference(x_prompt, x_sample, mem_prompt, cache_sb_k, cache_sb_v, page_table, cache_mem_k, cache_mem_v,
              w_in, b_gate, gmlp_ln_g, gmlp_ln_b, w_spatial, b_spatial, sb_bias, w_mem_kv, w_br_gmlp, w_br_sb,
              w_br_mem, w_out, ln1_g, ln1_b, peer_wq, peer_subkeys, peer_u, peer_v, ln2_g, ln2_b):
    n_seq = page_table.shape[0]
    x_p, x_s = x_prompt, x_sample
    sbk_p, sbv_p, mk_p, mv_p, sbk_s, sbv_s, gv_s = [], [], [], [], [], [], []
    for l in range(DEPTH):
        u, v, q, k, vv, qm, gates = in_proj(x_p, w_in[l], b_gate[l], gmlp_ln_g[l], gmlp_ln_b[l])
        mk, mv = mem_kv(mem_prompt, w_mem_kv[l])
        mix = merge(gates, spatial_gate(u, v, w_spatial[l], b_spatial[l]), sb_prompt(q, k, vv, sb_bias[l]),
                    mem_attend(qm, mk, mv), w_br_gmlp[l], w_br_sb[l], w_br_mem[l], w_out[l])
        x_p = layer_norm(DEEPNORM_ALPHA * x_p + mix, ln1_g[l], ln1_b[l])
        x_p = layer_norm(DEEPNORM_ALPHA * x_p + peer(x_p, peer_wq[l], peer_subkeys[l], peer_u[l], peer_v[l]),
                         ln2_g[l], ln2_b[l])
        sbk_p.append(k)
        sbv_p.append(vv)
        mk_p.append(mk)
        mv_p.append(mv)
        u, v, q, k, vv, qm, gates = in_proj(x_s, w_in[l], b_gate[l], gmlp_ln_g[l], gmlp_ln_b[l])
        k_past = cache_sb_k[l][page_table].reshape(n_seq, -1, SB_HEADS, SB_HD)
        v_past = cache_sb_v[l][page_table].reshape(n_seq, -1, SB_HEADS, SB_HD)
        mix = merge(gates, spatial_gate(u, v, w_spatial[l], b_spatial[l]),
                    sb_sample(q, k, vv, k_past, v_past, sb_bias[l]),
                    mem_attend(qm, cache_mem_k[l], cache_mem_v[l]), w_br_gmlp[l], w_br_sb[l], w_br_mem[l], w_out[l])
        x_s = layer_norm(DEEPNORM_ALPHA * x_s + mix, ln1_g[l], ln1_b[l])
        x_s = layer_norm(DEEPNORM_ALPHA * x_s + peer(x_s, peer_wq[l], peer_subkeys[l], peer_u[l], peer_v[l]),
                         ln2_g[l], ln2_b[l])
        sbk_s.append(k)
        sbv_s.append(vv)
        gv_s.append(v)
    return (x_p, x_s, jnp.stack(sbk_p), jnp.stack(sbv_p), jnp.stack(mk_p), jnp.stack(mv_p),
            jnp.stack(sbk_s), jnp.stack(sbv_s), jnp.stack(gv_s))
```

```python
import functools

import jax
import jax.numpy as jnp
from jax import lax
from jax.experimental import pallas as pl
from jax.experimental.pallas import tpu as pltpu

F32 = jnp.float32
BF16 = jnp.bfloat16

D_MODEL = 1024
G_WIDTH = 512
G_GROUPS = 4
CHUNK = 128
SB_HEADS = 8
SB_HD = 128
SB_WIDTH = 1024
MEM_HEADS = 4
MEM_HD = 128
MEM_WIDTH = 512
N_MEM = 256
N_BRANCH = 3
A_WIDTH = 2 * G_WIDTH + 3 * SB_WIDTH + MEM_WIDTH
PEER_HEADS = 8
PEER_DH = 128
N_KEYS = 128
PEER_TOPK = 16
PAGE = 128
LN_EPS = 1e-5
SB_SCALE = SB_HD ** -0.5
MEM_SCALE = MEM_HD ** -0.5

SB_BLOCK = 256
PAGES_PER_STEP = 4
VMEM_LIMIT = 52 * 1024 * 1024


def _dot(a, b):
    return jnp.dot(a, b, preferred_element_type=F32)


def _dot_nt(a, b):
    return lax.dot_general(a, b, (((1,), (1,)), ((), ())), preferred_element_type=F32)


def _layer_norm(x, g, b):
    mu = jnp.mean(x, axis=-1, keepdims=True)
    xc = x - mu
    var = jnp.mean(xc * xc, axis=-1, keepdims=True)
    return xc * lax.rsqrt(var + LN_EPS) * g + b


def _params(*sem):
    return pltpu.CompilerParams(dimension_semantics=sem, vmem_limit_bytes=VMEM_LIMIT)


def _proj_kernel(x_ref, w_ref, lng_ref, lnb_ref, ws_ref, bs_ref,
                 gout_ref, v_ref, qb_ref, k_ref, kb_ref, vv_ref, vvb_ref, qm_ref):
    tm = x_ref.shape[0]
    xb = x_ref[...].astype(BF16)
    u = jax.nn.gelu(_dot(xb, w_ref[:, 0:G_WIDTH]))
    gv = jax.nn.gelu(_dot(xb, w_ref[:, G_WIDTH:2 * G_WIDTH]))
    v = _layer_norm(gv, lng_ref[...], lnb_ref[...])
    v_ref[...] = v
    vb = v.astype(BF16)
    row = lax.broadcasted_iota(jnp.int32, (CHUNK, CHUNK), 0)
    col = lax.broadcasted_iota(jnp.int32, (CHUNK, CHUNK), 1)
    tril = col <= row
    for g in range(G_GROUPS):
        wg = jnp.where(tril, ws_ref[g], jnp.zeros((), BF16))
        cols = slice(g * CHUNK, (g + 1) * CHUNK)
        for c in range(tm // CHUNK):
            rows = slice(c * CHUNK, (c + 1) * CHUNK)
            s = _dot(wg, vb[rows, cols]) + bs_ref[:, cols]
            gout_ref[rows, cols] = (u[rows, cols] * s).astype(BF16)
    o = 2 * G_WIDTH
    qb_ref[...] = (_dot(xb, w_ref[:, o:o + SB_WIDTH]) * SB_SCALE).astype(BF16)
    k = _dot(xb, w_ref[:, o + SB_WIDTH:o + 2 * SB_WIDTH])
    k_ref[...] = k
    kb_ref[...] = k.astype(BF16)
    vv = _dot(xb, w_ref[:, o + 2 * SB_WIDTH:o + 3 * SB_WIDTH])
    vv_ref[...] = vv
    vvb_ref[...] = vv.astype(BF16)
    o = o + 3 * SB_WIDTH
    qm_ref[...] = (_dot(xb, w_ref[:, o:o + MEM_WIDTH]) * MEM_SCALE).astype(BF16)


def _proj(x2, w_a, lng, lnb, ws, bs_full, tm=256):
    n = x2.shape[0]
    row = lambda w: pl.BlockSpec((tm, w), lambda i: (i, 0))
    full = lambda a: pl.BlockSpec(a.shape, lambda i: (0,) * a.ndim)
    out_shape = (
        jax.ShapeDtypeStruct((n, G_WIDTH), BF16),
        jax.ShapeDtypeStruct((n, G_WIDTH), F32),
        jax.ShapeDtypeStruct((n, SB_WIDTH), BF16),
        jax.ShapeDtypeStruct((n, SB_WIDTH), F32),
        jax.ShapeDtypeStruct((n, SB_WIDTH), BF16),
        jax.ShapeDtypeStruct((n, SB_WIDTH), F32),
        jax.ShapeDtypeStruct((n, SB_WIDTH), BF16),
        jax.ShapeDtypeStruct((n, MEM_WIDTH), BF16),
    )
    return pl.pallas_call(
        _proj_kernel,
        grid=(n // tm,),
        in_specs=[row(D_MODEL), full(w_a), full(lng), full(lnb), full(ws), full(bs_full)],
        out_specs=tuple(row(s.shape[1]) for s in out_shape),
        out_shape=out_shape,
        compiler_params=_params("parallel"),
        name="proj",
    )(x2, w_a, lng, lnb, ws, bs_full)


def _matmul2_kernel(x_ref, w_ref, a_ref, b_ref):
    y = _dot(x_ref[...].astype(BF16), w_ref[...])
    h = a_ref.shape[1]
    a_ref[...] = y[:, :h]
    b_ref[...] = y[:, h:]


def _mem_kv(mem2, w, tm=256):
    n = mem2.shape[0]
    h = w.shape[1] // 2
    return pl.pallas_call(
        _matmul2_kernel,
        grid=(n // tm,),
        in_specs=[pl.BlockSpec((tm, D_MODEL), lambda i: (i, 0)), pl.BlockSpec(w.shape, lambda i: (0, 0))],
        out_specs=(pl.BlockSpec((tm, h), lambda i: (i, 0)), pl.BlockSpec((tm, h), lambda i: (i, 0))),
        out_shape=(jax.ShapeDtypeStruct((n, h), F32), jax.ShapeDtypeStruct((n, h), F32)),
        compiler_params=_params("parallel"),
        name="mem_kv",
    )(mem2, w)


def _sb_block(q, kb, vb, r_mat, bias, carry, causal):
    z = _dot_nt(q, kb) + bias
    lf = -(jnp.maximum(z, 0.0) + jnp.log(1.0 + jnp.exp(-jnp.abs(z))))
    if causal is not None:
        lf = jnp.where(causal, lf, 0.0)
    hi = lf.astype(BF16)
    lo = (lf - hi.astype(F32)).astype(BF16)
    cum = _dot(hi, r_mat) + _dot(lo, r_mat)
    a = jnp.exp(z + cum + carry)
    if causal is not None:
        a = jnp.where(causal, a, 0.0)
    return _dot(a.astype(BF16), vb), carry + cum[:, 0:1]


def _sb_prompt_kernel(bias_ref, q_ref, k_ref, v_ref, r_ref, o_ref):
    h = pl.program_id(1)
    i = pl.program_id(2)
    t = SB_BLOCK
    q = q_ref[...]
    r_mat = r_ref[...]
    bias = bias_ref[h]
    row = lax.broadcasted_iota(jnp.int32, (t, t), 0)
    col = lax.broadcasted_iota(jnp.int32, (t, t), 1)
    start = pl.multiple_of(i * t, t)
    acc, carry = _sb_block(q, k_ref[pl.ds(start, t), :], v_ref[pl.ds(start, t), :], r_mat, bias,
                           jnp.zeros((t, 1), F32), col < row)

    def body(j, st):
        acc, carry = st
        s0 = pl.multiple_of((i - 1 - j) * t, t)
        d, carry = _sb_block(q, k_ref[pl.ds(s0, t), :], v_ref[pl.ds(s0, t), :], r_mat, bias, carry, None)
        return acc + d, carry

    acc, _ = lax.fori_loop(0, i, body, (acc, carry))
    o_ref[...] = acc.astype(o_ref.dtype)


def _suffix_ones(n):
    j = lax.broadcasted_iota(jnp.int32, (n, n), 0)
    s = lax.broadcasted_iota(jnp.int32, (n, n), 1)
    return (j >= s).astype(BF16)


def _sb_prompt(qb, kb, vb, bias, batch, seq):
    t = SB_BLOCK
    nq = seq // t
    return pl.pallas_call(
        _sb_prompt_kernel,
        grid_spec=pltpu.PrefetchScalarGridSpec(
            num_scalar_prefetch=1,
            grid=(batch, SB_HEADS, nq),
            in_specs=[
                pl.BlockSpec((t, SB_HD), lambda b, h, i, bias: (b * nq + i, h)),
                pl.BlockSpec((seq, SB_HD), lambda b, h, i, bias: (b, h)),
                pl.BlockSpec((seq, SB_HD), lambda b, h, i, bias: (b, h)),
                pl.BlockSpec((t, t), lambda b, h, i, bias: (0, 0)),
            ],
            out_specs=pl.BlockSpec((t, SB_HD), lambda b, h, i, bias: (b * nq + i, h)),
        ),
        out_shape=jax.ShapeDtypeStruct(qb.shape, BF16),
        compiler_params=_params("parallel", "parallel", "arbitrary"),
        name="sb_prompt",
    )(bias, qb, kb, vb, _suffix_ones(t))


def _sb_sample_kernel(pt_ref, q_ref, kn_ref, vn_ref, *refs):
    npg = PAGES_PER_STEP
    k_refs = refs[:npg]
    v_refs = refs[npg:2 * npg]
    r_ref, bias_ref, o_ref, qbd_ref, acc_ref, carry_ref = refs[2 * npg:]
    s = pl.program_id(1)
    t = SB_BLOCK
    nrow = q_ref.shape[1] * SB_HEADS
    ntok = q_ref.shape[1]
    r_mat = r_ref[...]
    bias = bias_ref[...]

    def visit(kb, vb, causal):
        d, c = _sb_block(qbd_ref[...], kb, vb, r_mat, bias, carry_ref[:, 0:1], causal)
        acc_ref[...] += d
        carry_ref[...] = jnp.broadcast_to(c, carry_ref.shape)

    @pl.when(s == 0)
    def _():
        qrep = jnp.concatenate([q_ref[0]] * SB_HEADS, axis=0)
        rh = lax.broadcasted_iota(jnp.int32, qrep.shape, 0) // ntok
        ch = lax.broadcasted_iota(jnp.int32, qrep.shape, 1) // SB_HD
        qbd_ref[...] = jnp.where(rh == ch, qrep, 0.0).astype(BF16)
        acc_ref[...] = jnp.zeros_like(acc_ref)
        carry_ref[...] = jnp.zeros_like(carry_ref)
        tq = lax.broadcasted_iota(jnp.int32, (nrow, t), 0) % ntok
        tk = lax.broadcasted_iota(jnp.int32, (nrow, t), 1)
        visit(kn_ref[0], vn_ref[0], tk < tq)

    ppb = t // PAGE
    for blk in range(npg // ppb - 1, -1, -1):
        kb = jnp.concatenate([k_refs[blk * ppb + j][...] for j in range(ppb)], axis=0).astype(BF16)
        vb = jnp.concatenate([v_refs[blk * ppb + j][...] for j in range(ppb)], axis=0).astype(BF16)
        visit(kb, vb, None)

    @pl.when(s == pl.num_programs(1) - 1)
    def _():
        for h in range(SB_HEADS):
            cols = slice(h * SB_HD, (h + 1) * SB_HD)
            o_ref[0, :, cols] = acc_ref[h * ntok:(h + 1) * ntok, cols].astype(o_ref.dtype)


def _sb_sample(q3, kn_pad, vn_pad, cache_k, cache_v, page_table, bias_rows):
    nseq, ntok, _ = q3.shape
    npages = page_table.shape[1]
    npg = PAGES_PER_STEP
    nsteps = npages // npg
    t = SB_BLOCK
    nrow = ntok * SB_HEADS

    def page_spec(j):
        return pl.BlockSpec((None, PAGE, SB_WIDTH),
                            lambda b, s, pt: (pt[b, (nsteps - 1 - s) * npg + j], 0, 0))

    seq_spec = lambda r: pl.BlockSpec((1, r, SB_WIDTH), lambda b, s, pt: (b, 0, 0))
    return pl.pallas_call(
        _sb_sample_kernel,
        grid_spec=pltpu.PrefetchScalarGridSpec(
            num_scalar_prefetch=1,
            grid=(nseq, nsteps),
            in_specs=[seq_spec(ntok), seq_spec(t), seq_spec(t)]
            + [page_spec(j) for j in range(npg)] + [page_spec(j) for j in range(npg)]
            + [pl.BlockSpec((t, t), lambda b, s, pt: (0, 0)),
               pl.BlockSpec((nrow, 1), lambda b, s, pt: (0, 0))],
            out_specs=seq_spec(ntok),
            scratch_shapes=[pltpu.VMEM((nrow, SB_WIDTH), BF16),
                            pltpu.VMEM((nrow, SB_WIDTH), F32),
                            pltpu.VMEM((nrow, 128), F32)],
        ),
        out_shape=jax.ShapeDtypeStruct(q3.shape, F32),
        compiler_params=_params("parallel", "arbitrary"),
        name="sb_sample",
    )(page_table, q3, kn_pad, vn_pad, *([cache_k] * npg), *([cache_v] * npg), _suffix_ones(t), bias_rows)


def _mem_attn_kernel(q_ref, k_ref, v_ref, o_ref):
    for h in range(MEM_HEADS):
        cols = slice(h * MEM_HD, (h + 1) * MEM_HD)
        s = _dot_nt(q_ref[:, cols].astype(BF16), k_ref[:, cols].astype(BF16))
        e = jnp.exp(s - jnp.max(s, axis=-1, keepdims=True))
        p = e / jnp.sum(e, axis=-1, keepdims=True)
        o_ref[:, cols] = _dot(p.astype(BF16), v_ref[:, cols].astype(BF16)).astype(o_ref.dtype)


def _mem_attn(qm, mk3, mv3, tm):
    n = qm.shape[0]
    nb = mk3.shape[0]
    nt = n // (nb * tm)
    kv = pl.BlockSpec((None, N_MEM, MEM_WIDTH), lambda b, i: (b, 0, 0))
    return pl.pallas_call(
        _mem_attn_kernel,
        grid=(nb, nt),
        in_specs=[pl.BlockSpec((tm, MEM_WIDTH), lambda b, i: (b * nt + i, 0)), kv, kv],
        out_specs=pl.BlockSpec((tm, MEM_WIDTH), lambda b, i: (b * nt + i, 0)),
        out_shape=jax.ShapeDtypeStruct(qm.shape, qm.dtype),
        compiler_params=_params("parallel", "parallel"),
        name="mem_attn",
    )(qm, mk3, mv3)


def _merge_kernel(alpha, x_ref, g_ref, sb_ref, m_ref, wg_ref, bg_ref, wbg_ref, wbs_ref, wbm_ref, wo_ref,
                  lng_ref, lnb_ref, o_ref):
    x = x_ref[...]
    gates = jax.nn.sigmoid(_dot(x.astype(BF16), wg_ref[...]) + bg_ref[...])
    d = D_MODEL
    z = (gates[:, 0:d] * _dot(g_ref[...], wbg_ref[...])
         + gates[:, d:2 * d] * _dot(sb_ref[...], wbs_ref[...])
         + gates[:, 2 * d:3 * d] * _dot(m_ref[...], wbm_ref[...]))
    mix = _dot(z.astype(BF16), wo_ref[...])
    o_ref[...] = _layer_norm(alpha * x + mix, lng_ref[...], lnb_ref[...])


def _merge(alpha, x2, gout, sb, mo, w_gate, b_gate, w_bg, w_bs, w_bm, w_o, lng, lnb, tm=256):
    n = x2.shape[0]
    row = lambda a: pl.BlockSpec((tm, a.shape[1]), lambda i: (i, 0))
    full = lambda a: pl.BlockSpec(a.shape, lambda i: (0,) * a.ndim)
    ws = (w_gate, b_gate, w_bg, w_bs, w_bm, w_o, lng, lnb)
    return pl.pallas_call(
        functools.partial(_merge_kernel, alpha),
        grid=(n // tm,),
        in_specs=[row(x2), row(gout), row(sb), row(mo)] + [full(a) for a in ws],
        out_specs=row(x2),
        out_shape=jax.ShapeDtypeStruct(x2.shape, F32),
        compiler_params=_params("parallel"),
        name="merge",
    )(x2, gout, sb, mo, *ws)


def _cmpx(v, i, j):
    hi = jnp.maximum(v[i], v[j])
    lo = jnp.minimum(v[i], v[j])
    v[i], v[j] = hi, lo


def _sort16_desc(v):
    v = list(v)
    n = len(v)
    k = 2
    while k <= n:
        j = k // 2
        while j >= 1:
            for i in range(n):
                l = i ^ j
                if l > i:
                    if (i & k) == 0:
                        _cmpx(v, i, l)
                    else:
                        _cmpx(v, l, i)
            j //= 2
        k *= 2
    return v


def _merge_top16(a, b):
    n = len(a)
    c = []
    for i in range(n):
        j = n - 1 - i
        c.append(jnp.maximum(a[i], b[j]) if j < len(b) else a[i])
    j = n // 2
    while j >= 1:
        for i in range(n):
            if (i & j) == 0:
                _cmpx(c, i, i + j)
        j //= 2
    return c


def _top16_rows(s):
    g = _sort16_desc([s[8 * i:8 * i + 8, :] for i in range(N_KEYS // 8)])
    for shift in (4, 2, 1):
        g = _merge_top16(g, [pltpu.roll(x, shift, 0) for x in g])
    return g


def _tile_rows(a, reps):
    return jnp.concatenate([a] * reps, axis=0)


def _peer_score_kernel(x_ref, wq_ref, sk_ref, xt_ref, s1_ref, c_ref, s2_ref, p_ref, tau_ref):
    xt = x_ref[...].T.astype(BF16)
    xt_ref[...] = xt
    k = PEER_TOPK

    def head(h, carry):
        qt = _dot(wq_ref[pl.ds(pl.multiple_of(h * 2 * PEER_DH, 2 * PEER_DH), 2 * PEER_DH), :], xt)
        s1 = _dot(sk_ref[2 * h], qt[:PEER_DH].astype(BF16))
        s2 = _dot(sk_ref[2 * h + 1], qt[PEER_DH:].astype(BF16))
        t1 = _top16_rows(s1)
        t2 = _top16_rows(s2)
        top = [t1[0] + t2[b] for b in range(k)]
        for a in range(1, k):
            top = _merge_top16(top, [t1[a] + t2[b] for b in range(k // (a + 1))])
        tau = top[k - 1]
        m = t1[0] + t2[0]
        zsum = jnp.zeros_like(m)
        for a in range(k):
            for b in range(k // (a + 1)):
                c = t1[a] + t2[b]
                zsum = zsum + jnp.where(c >= tau, jnp.exp(c - m), 0.0)
        reps = N_KEYS // 8
        s1_ref[h] = s1
        s2_ref[h] = s2
        c_ref[h] = jnp.exp(s1 - _tile_rows(t1[0], reps)) / _tile_rows(zsum, reps)
        p_ref[h] = jnp.exp(s2 - _tile_rows(t2[0], reps))
        tau_ref[h] = tau
        return carry

    lax.fori_loop(0, PEER_HEADS, head, 0)


def _peer_score(x1, wq_t, sk, tm):
    n = x1.shape[0]
    big = jax.ShapeDtypeStruct((PEER_HEADS, N_KEYS, n), F32)
    bspec = pl.BlockSpec((PEER_HEADS, N_KEYS, tm), lambda i: (0, 0, i))
    return pl.pallas_call(
        _peer_score_kernel,
        grid=(n // tm,),
        in_specs=[pl.BlockSpec((tm, D_MODEL), lambda i: (i, 0)),
                  pl.BlockSpec(wq_t.shape, lambda i: (0, 0)),
                  pl.BlockSpec(sk.shape, lambda i: (0, 0, 0))],
        out_specs=(pl.BlockSpec((D_MODEL, tm), lambda i: (0, i)), bspec, bspec, bspec, bspec,
                   pl.BlockSpec((PEER_HEADS, 8, tm), lambda i: (0, 0, i))),
        out_shape=(jax.ShapeDtypeStruct((D_MODEL, n), BF16), big, big, big, big,
                   jax.ShapeDtypeStruct((PEER_HEADS, 8, n), F32)),
        compiler_params=_params("parallel"),
        name="peer_score",
    )(x1, wq_t, sk)


EXPERT_STEP = 1024
EXPERT_SUB = 256


def _peer_dense_kernel(alpha, x_ref, xt_ref, u_ref, vt_ref, s1_ref, c_ref, s2_ref, p_ref, tau_ref,
                       lng_ref, lnb_ref, o_ref, acc_ref, act_ref, w_ref):
    e = pl.program_id(1)
    tm = x_ref.shape[0]

    @pl.when(e == 0)
    def _():
        acc_ref[...] = jnp.zeros_like(acc_ref)

    xt = xt_ref[...]
    for sub in range(EXPERT_STEP // EXPERT_SUB):
        rows = slice(sub * EXPERT_SUB, (sub + 1) * EXPERT_SUB)
        act_ref[...] = jax.nn.gelu(_dot(u_ref[rows, :], xt))
        for half in range(EXPERT_SUB // N_KEYS):
            r = sub * (EXPERT_SUB // N_KEYS) + half
            er = slice(half * N_KEYS, (half + 1) * N_KEYS)
            for cc in range(tm // 128):
                cs = slice(cc * 128, (cc + 1) * 128)
                g = jnp.zeros((N_KEYS, 128), F32)
                for h in range(PEER_HEADS):
                    ssum = s2_ref[h, :, cs] + s1_ref[h, r:r + 1, cs]
                    g = g + jnp.where(ssum >= tau_ref[h, 0:1, cs], p_ref[h, :, cs] * c_ref[h, r:r + 1, cs], 0.0)
                w_ref[er, cs] = (g * act_ref[er, cs]).astype(BF16)
        acc_ref[...] += _dot(vt_ref[:, rows], w_ref[...])

    @pl.when(e == pl.num_programs(1) - 1)
    def _():
        y = acc_ref[...].T
        o_ref[...] = _layer_norm(alpha * x_ref[...] + y, lng_ref[...], lnb_ref[...])


def _peer_dense(alpha, x1, xt, u_b, vt_b, s1, c, s2, p, tau, lng, lnb, tm):
    n = x1.shape[0]
    ne = u_b.shape[0] // EXPERT_STEP
    big = pl.BlockSpec((PEER_HEADS, N_KEYS, tm), lambda i, e: (0, 0, i))
    rowsel = pl.BlockSpec((PEER_HEADS, 8, tm), lambda i, e: (0, e, i))
    return pl.pallas_call(
        functools.partial(_peer_dense_kernel, alpha),
        grid=(n // tm, ne),
        in_specs=[pl.BlockSpec((tm, D_MODEL), lambda i, e: (i, 0)),
                  pl.BlockSpec((D_MODEL, tm), lambda i, e: (0, i)),
                  pl.BlockSpec((EXPERT_STEP, D_MODEL), lambda i, e: (e, 0)),
                  pl.BlockSpec((D_MODEL, EXPERT_STEP), lambda i, e: (0, e)),
                  rowsel, rowsel, big, big,
                  pl.BlockSpec((PEER_HEADS, 8, tm), lambda i, e: (0, 0, i)),
                  pl.BlockSpec(lng.shape, lambda i, e: (0, 0)),
                  pl.BlockSpec(lnb.shape, lambda i, e: (0, 0))],
        out_specs=pl.BlockSpec((tm, D_MODEL), lambda i, e: (i, 0)),
        out_shape=jax.ShapeDtypeStruct(x1.shape, F32),
        scratch_shapes=[pltpu.VMEM((D_MODEL, tm), F32),
                        pltpu.VMEM((EXPERT_SUB, tm), F32),
                        pltpu.VMEM((EXPERT_SUB, tm), BF16)],
        compiler_params=_params("parallel", "arbitrary"),
        name="peer_dense",
    )(x1, xt, u_b, vt_b, s1, c, s2, p, tau, lng, lnb)


def _peer(alpha, x1, wq_t, sk, u_b, vt_b, lng, lnb, tm):
    xt, s1, c, s2, p, tau = _peer_score(x1, wq_t, sk, tm)
    return _peer_dense(alpha, x1, xt, u_b, vt_b, s1, c, s2, p, tau, lng, lnb, tm)


def kernel(x_prompt, x_sample, mem_prompt, cache_sb_k, cache_sb_v, page_table, cache_mem_k, cache_mem_v,
           w_in, b_gate, gmlp_ln_g, gmlp_ln_b, w_spatial, b_spatial, sb_bias, w_mem_kv, w_br_gmlp, w_br_sb,
           w_br_mem, w_out, ln1_g, ln1_b, peer_wq, peer_subkeys, peer_u, peer_v, ln2_g, ln2_b):
    depth = w_in.shape[0]
    alpha = float((2 * depth) ** 0.25)
    batch, seq, d = x_prompt.shape
    nseq, ntok, _ = x_sample.shape
    n_phys = cache_sb_k.shape[1]
    assert seq % SB_BLOCK == 0 and (nseq * ntok) % 256 == 0 and CHUNK % ntok == 0

    xp = x_prompt.reshape(batch * seq, d)
    xs = x_sample.reshape(nseq * ntok, d)
    outs = [[] for _ in range(7)]
    row2 = lambda a: a.reshape(1, -1)
    for l in range(depth):
        w_a = w_in[l][:, :A_WIDTH].astype(BF16)
        w_gate = w_in[l][:, A_WIDTH:].astype(BF16)
        bg = b_gate[l].reshape(1, N_BRANCH * D_MODEL)
        lng, lnb = row2(gmlp_ln_g[l]), row2(gmlp_ln_b[l])
        ws_p = w_spatial[l].astype(BF16)
        bs_p = jnp.repeat(b_spatial[l].T, CHUNK, axis=1)
        reps = CHUNK // ntok
        eye = jnp.eye(reps, dtype=F32)
        ws_s = jnp.einsum('ab,gij->gaibj', eye, w_spatial[l][:, :ntok, :ntok]).reshape(G_GROUPS, CHUNK, CHUNK).astype(BF16)
        bs_s = jnp.tile(bs_p[:ntok], (reps, 1))
        w_bg, w_bs, w_bm, w_o = (w.astype(BF16) for w in (w_br_gmlp[l], w_br_sb[l], w_br_mem[l], w_out[l]))
        l1g, l1b, l2g, l2b = row2(ln1_g[l]), row2(ln1_b[l]), row2(ln2_g[l]), row2(ln2_b[l])
        wq_t = peer_wq[l].T.astype(BF16)
        sk = peer_subkeys[l].reshape(PEER_HEADS * 2, N_KEYS, PEER_DH).astype(BF16)
        u_b = peer_u[l].astype(BF16)
        vt_b = peer_v[l].T.astype(BF16)
        bias = sb_bias[l].astype(F32)

        gout, _, qb, k, kb, vv, vvb, qm = _proj(xp, w_a, lng, lnb, ws_p, bs_p)
        mk, mv = _mem_kv(mem_prompt.reshape(-1, d), w_mem_kv[l].astype(BF16))
        mk3, mv3 = mk.reshape(batch, N_MEM, MEM_WIDTH), mv.reshape(batch, N_MEM, MEM_WIDTH)
        sb = _sb_prompt(qb, kb, vvb, bias, batch, seq)
        mo = _mem_attn(qm, mk3, mv3, tm=512)
        x1 = _merge(alpha, xp, gout, sb, mo, w_gate, bg, w_bg, w_bs, w_bm, w_o, l1g, l1b)
        xp = _peer(alpha, x1, wq_t, sk, u_b, vt_b, l2g, l2b, tm=512)
        outs[0].append(k.reshape(batch, seq, SB_HEADS, SB_HD))
        outs[1].append(vv.reshape(batch, seq, SB_HEADS, SB_HD))
        outs[2].append(mk.reshape(batch, N_MEM, MEM_HEADS, MEM_HD))
        outs[3].append(mv.reshape(batch, N_MEM, MEM_HEADS, MEM_HD))

        gout, v, qb, k, kb, vv, vvb, qm = _proj(xs, w_a, lng, lnb, ws_s, bs_s)
        pad = lambda a: jnp.pad(a.reshape(nseq, ntok, SB_WIDTH), ((0, 0), (0, SB_BLOCK - ntok), (0, 0)))
        bias_rows = jnp.repeat(bias, ntok).reshape(SB_HEADS * ntok, 1)
        sb = _sb_sample(qb.astype(F32).reshape(nseq, ntok, SB_WIDTH), pad(kb), pad(vvb),
                        cache_sb_k[l].reshape(n_phys, PAGE, SB_WIDTH), cache_sb_v[l].reshape(n_phys, PAGE, SB_WIDTH),
                        page_table, bias_rows).reshape(nseq * ntok, SB_WIDTH).astype(BF16)
        mo = _mem_attn(qm.astype(F32), cache_mem_k[l].reshape(nseq, N_MEM, MEM_WIDTH),
                       cache_mem_v[l].reshape(nseq, N_MEM, MEM_WIDTH), tm=ntok).astype(BF16)
        x1 = _merge(alpha, xs, gout, sb, mo, w_gate, bg, w_bg, w_bs, w_bm, w_o, l1g, l1b)
        xs = _peer(alpha, x1, wq_t, sk, u_b, vt_b, l2g, l2b, tm=256)
        outs[4].append(k.reshape(nseq, ntok, SB_HEADS, SB_HD))
        outs[5].append(vv.reshape(nseq, ntok, SB_HEADS, SB_HD))
        outs[6].append(v.reshape(nseq, ntok, G_GROUPS, CHUNK))

    return (xp.reshape(batch, seq, d), xs.reshape(nseq, ntok, d)) + tuple(jnp.stack(o) for o in outs)
```

```python
import functools

import jax
import jax.numpy as jnp
from jax import lax
from jax.experimental import pallas as pl
from jax.experimental.pallas import tpu as pltpu

F32 = jnp.float32
BF16 = jnp.bfloat16

D_MODEL = 1024
G_WIDTH = 512
G_GROUPS = 4
CHUNK = 128
SB_HEADS = 8
SB_HD = 128
SB_WIDTH = 1024
MEM_HEADS = 4
MEM_HD = 128
MEM_WIDTH = 512
N_MEM = 256
N_BRANCH = 3
A_WIDTH = 2 * G_WIDTH + 3 * SB_WIDTH + MEM_WIDTH
PEER_HEADS = 8
PEER_DH = 128
N_KEYS = 128
PEER_TOPK = 16
PAGE = 128
LN_EPS = 1e-5
SB_SCALE = SB_HD ** -0.5
MEM_SCALE = MEM_HD ** -0.5

SB_CHUNK = 256
SB_BLOCK = 512
SB_HP = 2
PAGES_PER_STEP = 8
VMEM_LIMIT = 52 * 1024 * 1024


def _dot(a, b):
    return jnp.dot(a, b, preferred_element_type=F32)


def _dot_nt(a, b):
    return lax.dot_general(a, b, (((1,), (1,)), ((), ())), preferred_element_type=F32)


def _layer_norm(x, g, b):
    mu = jnp.mean(x, axis=-1, keepdims=True)
    xc = x - mu
    var = jnp.mean(xc * xc, axis=-1, keepdims=True)
    return xc * lax.rsqrt(var + LN_EPS) * g + b


def _params(*sem):
    return pltpu.CompilerParams(dimension_semantics=sem, vmem_limit_bytes=VMEM_LIMIT)


def _proj_kernel(x_ref, w_ref, lng_ref, lnb_ref, ws_ref, bs_ref,
                 gout_ref, v_ref, qb_ref, k_ref, kb_ref, vv_ref, vvb_ref, qm_ref):
    tm = x_ref.shape[0]
    xb = x_ref[...].astype(BF16)
    u = jax.nn.gelu(_dot(xb, w_ref[:, 0:G_WIDTH]))
    gv = jax.nn.gelu(_dot(xb, w_ref[:, G_WIDTH:2 * G_WIDTH]))
    v = _layer_norm(gv, lng_ref[...], lnb_ref[...])
    v_ref[...] = v
    vb = v.astype(BF16)
    row = lax.broadcasted_iota(jnp.int32, (CHUNK, CHUNK), 0)
    col = lax.broadcasted_iota(jnp.int32, (CHUNK, CHUNK), 1)
    tril = col <= row
    for g in range(G_GROUPS):
        wg = jnp.where(tril, ws_ref[g], jnp.zeros((), BF16))
        cols = slice(g * CHUNK, (g + 1) * CHUNK)
        for c in range(tm // CHUNK):
            rows = slice(c * CHUNK, (c + 1) * CHUNK)
            s = _dot(wg, vb[rows, cols]) + bs_ref[:, cols]
            gout_ref[rows, cols] = (u[rows, cols] * s).astype(BF16)
    o = 2 * G_WIDTH
    qb_ref[...] = (_dot(xb, w_ref[:, o:o + SB_WIDTH]) * SB_SCALE).astype(BF16)
    k = _dot(xb, w_ref[:, o + SB_WIDTH:o + 2 * SB_WIDTH])
    k_ref[...] = k
    kb_ref[...] = k.astype(BF16)
    vv = _dot(xb, w_ref[:, o + 2 * SB_WIDTH:o + 3 * SB_WIDTH])
    vv_ref[...] = vv
    vvb_ref[...] = vv.astype(BF16)
    o = o + 3 * SB_WIDTH
    qm_ref[...] = (_dot(xb, w_ref[:, o:o + MEM_WIDTH]) * MEM_SCALE).astype(BF16)


def _proj(x2, w_a, lng, lnb, ws, bs_full, tm=256):
    n = x2.shape[0]
    row = lambda w: pl.BlockSpec((tm, w), lambda i: (i, 0))
    full = lambda a: pl.BlockSpec(a.shape, lambda i: (0,) * a.ndim)
    out_shape = (
        jax.ShapeDtypeStruct((n, G_WIDTH), BF16),
        jax.ShapeDtypeStruct((n, G_WIDTH), F32),
        jax.ShapeDtypeStruct((n, SB_WIDTH), BF16),
        jax.ShapeDtypeStruct((n, SB_WIDTH), F32),
        jax.ShapeDtypeStruct((n, SB_WIDTH), BF16),
        jax.ShapeDtypeStruct((n, SB_WIDTH), F32),
        jax.ShapeDtypeStruct((n, SB_WIDTH), BF16),
        jax.ShapeDtypeStruct((n, MEM_WIDTH), BF16),
    )
    return pl.pallas_call(
        _proj_kernel,
        grid=(n // tm,),
        in_specs=[row(D_MODEL), full(w_a), full(lng), full(lnb), full(ws), full(bs_full)],
        out_specs=tuple(row(s.shape[1]) for s in out_shape),
        out_shape=out_shape,
        compiler_params=_params("parallel"),
        name="proj",
    )(x2, w_a, lng, lnb, ws, bs_full)


def _matmul2_kernel(x_ref, w_ref, a_ref, b_ref):
    y = _dot(x_ref[...].astype(BF16), w_ref[...])
    h = a_ref.shape[1]
    a_ref[...] = y[:, :h]
    b_ref[...] = y[:, h:]


def _mem_kv(mem2, w, tm=256):
    n = mem2.shape[0]
    h = w.shape[1] // 2
    return pl.pallas_call(
        _matmul2_kernel,
        grid=(n // tm,),
        in_specs=[pl.BlockSpec((tm, D_MODEL), lambda i: (i, 0)), pl.BlockSpec(w.shape, lambda i: (0, 0))],
        out_specs=(pl.BlockSpec((tm, h), lambda i: (i, 0)), pl.BlockSpec((tm, h), lambda i: (i, 0))),
        out_shape=(jax.ShapeDtypeStruct((n, h), F32), jax.ShapeDtypeStruct((n, h), F32)),
        compiler_params=_params("parallel"),
        name="mem_kv",
    )(mem2, w)


def _sb_block(q, kb, vb, r_mat, bias, carry, causal):
    c = SB_CHUNK
    z = _dot_nt(q, kb) + bias
    sp = jnp.maximum(z, 0.0) + jnp.log(1.0 + jnp.exp(-jnp.abs(z)))
    if causal is not None:
        sp = jnp.where(causal, sp, 0.0)
    spb = sp.astype(BF16)
    nchunk = kb.shape[0] // c
    local = [_dot(spb[:, j * c:(j + 1) * c], r_mat) for j in range(nchunk)]
    pieces = [None] * nchunk
    for j in range(nchunk - 1, -1, -1):
        pieces[j] = jnp.exp(z[:, j * c:(j + 1) * c] - local[j] - carry)
        carry = carry + local[j][:, 0:1]
    a = pieces[0] if nchunk == 1 else jnp.concatenate(pieces, axis=1)
    if causal is not None:
        a = jnp.where(causal, a, 0.0)
    return _dot(a.astype(BF16), vb), carry


def _sb_prompt_kernel(bias_ref, q_ref, k_ref, v_ref, r_ref, o_ref):
    hg = pl.program_id(1)
    i = pl.program_id(2)
    t = SB_BLOCK
    r_mat = r_ref[...]
    row = lax.broadcasted_iota(jnp.int32, (t, t), 0)
    col = lax.broadcasted_iota(jnp.int32, (t, t), 1)
    heads = [(slice(j * SB_HD, (j + 1) * SB_HD), bias_ref[hg * SB_HP + j]) for j in range(SB_HP)]

    def visit(start, state, causal):
        out = []
        for (cols, bias), (acc, carry) in zip(heads, state):
            d, carry = _sb_block(q_ref[:, cols], k_ref[pl.ds(start, t), cols], v_ref[pl.ds(start, t), cols],
                                 r_mat, bias, carry, causal)
            out.append((acc + d, carry))
        return tuple(out)

    zero = (jnp.zeros((t, SB_HD), F32), jnp.zeros((t, 1), F32))
    state = visit(pl.multiple_of(i * t, t), (zero,) * SB_HP, col < row)
    state = lax.fori_loop(0, i, lambda j, st: visit(pl.multiple_of((i - 1 - j) * t, t), st, None), state)
    for (cols, _), (acc, _) in zip(heads, state):
        o_ref[:, cols] = acc.astype(o_ref.dtype)


def _suffix_ones(n):
    j = lax.broadcasted_iota(jnp.int32, (n, n), 0)
    s = lax.broadcasted_iota(jnp.int32, (n, n), 1)
    return (j >= s).astype(BF16)


def _sb_prompt(qb, kb, vb, bias, batch, seq):
    t = SB_BLOCK
    nq = seq // t
    w = SB_HP * SB_HD
    return pl.pallas_call(
        _sb_prompt_kernel,
        grid_spec=pltpu.PrefetchScalarGridSpec(
            num_scalar_prefetch=1,
            grid=(batch, SB_HEADS // SB_HP, nq),
            in_specs=[
                pl.BlockSpec((t, w), lambda b, h, i, bias: (b * nq + i, h)),
                pl.BlockSpec((seq, w), lambda b, h, i, bias: (b, h)),
                pl.BlockSpec((seq, w), lambda b, h, i, bias: (b, h)),
                pl.BlockSpec((SB_CHUNK, SB_CHUNK), lambda b, h, i, bias: (0, 0)),
            ],
            out_specs=pl.BlockSpec((t, w), lambda b, h, i, bias: (b * nq + i, h)),
        ),
        out_shape=jax.ShapeDtypeStruct(qb.shape, BF16),
        compiler_params=_params("parallel", "parallel", "arbitrary"),
        name="sb_prompt",
    )(bias, qb, kb, vb, _suffix_ones(SB_CHUNK))


def _sb_sample_kernel(pt_ref, q_ref, kn_ref, vn_ref, *refs):
    npg = PAGES_PER_STEP
    k_refs = refs[:npg]
    v_refs = refs[npg:2 * npg]
    r_ref, bias_ref, o_ref, qbd_ref, acc_ref, carry_ref = refs[2 * npg:]
    s = pl.program_id(1)
    t = kn_ref.shape[1]
    nrow = q_ref.shape[1] * SB_HEADS
    ntok = q_ref.shape[1]
    r_mat = r_ref[...]
    bias = bias_ref[...]

    def visit(kb, vb, causal):
        d, c = _sb_block(qbd_ref[...], kb, vb, r_mat, bias, carry_ref[:, 0:1], causal)
        acc_ref[...] += d
        carry_ref[...] = jnp.broadcast_to(c, carry_ref.shape)

    @pl.when(s == 0)
    def _():
        qrep = jnp.concatenate([q_ref[0]] * SB_HEADS, axis=0)
        rh = lax.broadcasted_iota(jnp.int32, qrep.shape, 0) // ntok
        ch = lax.broadcasted_iota(jnp.int32, qrep.shape, 1) // SB_HD
        qbd_ref[...] = jnp.where(rh == ch, qrep, 0.0).astype(BF16)
        acc_ref[...] = jnp.zeros_like(acc_ref)
        carry_ref[...] = jnp.zeros_like(carry_ref)
        tq = lax.broadcasted_iota(jnp.int32, (nrow, t), 0) % ntok
        tk = lax.broadcasted_iota(jnp.int32, (nrow, t), 1)
        visit(kn_ref[0], vn_ref[0], tk < tq)

    def rows(page_ref):
        return jnp.concatenate([page_ref[pl.ds(h, PAGE, stride=SB_HEADS), :] for h in range(SB_HEADS)],
                               axis=1).astype(BF16)

    visit(jnp.concatenate([rows(r) for r in k_refs], axis=0), jnp.concatenate([rows(r) for r in v_refs], axis=0), None)

    @pl.when(s == pl.num_programs(1) - 1)
    def _():
        for h in range(SB_HEADS):
            cols = slice(h * SB_HD, (h + 1) * SB_HD)
            o_ref[0, :, cols] = acc_ref[h * ntok:(h + 1) * ntok, cols].astype(o_ref.dtype)


def _sb_sample(q3, kn_pad, vn_pad, cache_k, cache_v, layer, page_table, bias_rows):
    nseq, ntok, _ = q3.shape
    npages = page_table.shape[1]
    npg = PAGES_PER_STEP
    nsteps = npages // npg
    t = kn_pad.shape[1]
    nrow = ntok * SB_HEADS

    def page_spec(j):
        return pl.BlockSpec((None, None, PAGE * SB_HEADS, SB_HD),
                            lambda b, s, pt: (layer, pt[b, (nsteps - 1 - s) * npg + j], 0, 0))

    seq_spec = lambda r: pl.BlockSpec((1, r, SB_WIDTH), lambda b, s, pt: (b, 0, 0))
    return pl.pallas_call(
        _sb_sample_kernel,
        grid_spec=pltpu.PrefetchScalarGridSpec(
            num_scalar_prefetch=1,
            grid=(nseq, nsteps),
            in_specs=[seq_spec(ntok), seq_spec(t), seq_spec(t)]
            + [page_spec(j) for j in range(npg)] + [page_spec(j) for j in range(npg)]
            + [pl.BlockSpec((SB_CHUNK, SB_CHUNK), lambda b, s, pt: (0, 0)),
               pl.BlockSpec((nrow, 1), lambda b, s, pt: (0, 0))],
            out_specs=seq_spec(ntok),
            scratch_shapes=[pltpu.VMEM((nrow, SB_WIDTH), BF16),
                            pltpu.VMEM((nrow, SB_WIDTH), F32),
                            pltpu.VMEM((nrow, 128), F32)],
        ),
        out_shape=jax.ShapeDtypeStruct(q3.shape, F32),
        compiler_params=_params("parallel", "arbitrary"),
        name="sb_sample",
    )(page_table, q3, kn_pad, vn_pad, *([cache_k] * npg), *([cache_v] * npg), _suffix_ones(SB_CHUNK), bias_rows)


def _mem_attn_kernel(q_ref, k_ref, v_ref, o_ref):
    for h in range(MEM_HEADS):
        cols = slice(h * MEM_HD, (h + 1) * MEM_HD)
        s = _dot_nt(q_ref[:, cols].astype(BF16), k_ref[:, cols].astype(BF16))
        e = jnp.exp(s - jnp.max(s, axis=-1, keepdims=True))
        p = e / jnp.sum(e, axis=-1, keepdims=True)
        o_ref[:, cols] = _dot(p.astype(BF16), v_ref[:, cols].astype(BF16)).astype(o_ref.dtype)


def _mem_attn(qm, mk3, mv3, tm):
    n = qm.shape[0]
    nb = mk3.shape[0]
    nt = n // (nb * tm)
    kv = pl.BlockSpec((None, N_MEM, MEM_WIDTH), lambda b, i: (b, 0, 0))
    return pl.pallas_call(
        _mem_attn_kernel,
        grid=(nb, nt),
        in_specs=[pl.BlockSpec((tm, MEM_WIDTH), lambda b, i: (b * nt + i, 0)), kv, kv],
        out_specs=pl.BlockSpec((tm, MEM_WIDTH), lambda b, i: (b * nt + i, 0)),
        out_shape=jax.ShapeDtypeStruct(qm.shape, qm.dtype),
        compiler_params=_params("parallel", "parallel"),
        name="mem_attn",
    )(qm, mk3, mv3)


def _merge_kernel(alpha, x_ref, g_ref, sb_ref, m_ref, wg_ref, bg_ref, wbg_ref, wbs_ref, wbm_ref, wo_ref,
                  lng_ref, lnb_ref, o_ref):
    x = x_ref[...]
    gates = jax.nn.sigmoid(_dot(x.astype(BF16), wg_ref[...]) + bg_ref[...])
    d = D_MODEL
    z = (gates[:, 0:d] * _dot(g_ref[...], wbg_ref[...])
         + gates[:, d:2 * d] * _dot(sb_ref[...], wbs_ref[...])
         + gates[:, 2 * d:3 * d] * _dot(m_ref[...], wbm_ref[...]))
    mix = _dot(z.astype(BF16), wo_ref[...])
    o_ref[...] = _layer_norm(alpha * x + mix, lng_ref[...], lnb_ref[...])


def _merge(alpha, x2, gout, sb, mo, w_gate, b_gate, w_bg, w_bs, w_bm, w_o, lng, lnb, tm=256):
    n = x2.shape[0]
    row = lambda a: pl.BlockSpec((tm, a.shape[1]), lambda i: (i, 0))
    full = lambda a: pl.BlockSpec(a.shape, lambda i: (0,) * a.ndim)
    ws = (w_gate, b_gate, w_bg, w_bs, w_bm, w_o, lng, lnb)
    return pl.pallas_call(
        functools.partial(_merge_kernel, alpha),
        grid=(n // tm,),
        in_specs=[row(x2), row(gout), row(sb), row(mo)] + [full(a) for a in ws],
        out_specs=row(x2),
        out_shape=jax.ShapeDtypeStruct(x2.shape, F32),
        compiler_params=_params("parallel"),
        name="merge",
    )(x2, gout, sb, mo, *ws)


def _cmpx(v, i, j):
    hi = jnp.maximum(v[i], v[j])
    lo = jnp.minimum(v[i], v[j])
    v[i], v[j] = hi, lo


def _sort16_desc(v):
    v = list(v)
    n = len(v)
    k = 2
    while k <= n:
        j = k // 2
        while j >= 1:
            for i in range(n):
                l = i ^ j
                if l > i:
                    if (i & k) == 0:
                        _cmpx(v, i, l)
                    else:
                        _cmpx(v, l, i)
            j //= 2
        k *= 2
    return v


def _merge_top16(a, b):
    n = len(a)
    c = []
    for i in range(n):
        j = n - 1 - i
        c.append(jnp.maximum(a[i], b[j]) if j < len(b) else a[i])
    j = n // 2
    while j >= 1:
        for i in range(n):
            if (i & j) == 0:
                _cmpx(c, i, i + j)
        j //= 2
    return c


def _top16_rows(s):
    g = _sort16_desc([s[8 * i:8 * i + 8, :] for i in range(N_KEYS // 8)])
    for shift in (4, 2, 1):
        g = _merge_top16(g, [pltpu.roll(x, shift, 0) for x in g])
    return g


def _tile_rows(a, reps):
    return jnp.concatenate([a] * reps, axis=0)


def _peer_score_kernel(x_ref, wq_ref, sk_ref, xt_ref, nsel_ref, c_ref, rank_ref, p_ref):
    xt = x_ref[...].T.astype(BF16)
    xt_ref[...] = xt
    k = PEER_TOPK

    def head(h, carry):
        qt = _dot(wq_ref[pl.ds(pl.multiple_of(h * 2 * PEER_DH, 2 * PEER_DH), 2 * PEER_DH), :], xt)
        s1 = _dot(sk_ref[2 * h], qt[:PEER_DH].astype(BF16))
        s2 = _dot(sk_ref[2 * h + 1], qt[PEER_DH:].astype(BF16))
        t1 = _top16_rows(s1)
        t2 = _top16_rows(s2)
        top = [t1[0] + t2[b] for b in range(k)]
        for a in range(1, k):
            top = _merge_top16(top, [t1[a] + t2[b] for b in range(k // (a + 1))])
        tau = top[k - 1]
        m = t1[0] + t2[0]
        zsum = jnp.zeros_like(m)
        for a in range(k):
            for b in range(k // (a + 1)):
                c = t1[a] + t2[b]
                zsum = zsum + jnp.where(c >= tau, jnp.exp(c - m), 0.0)
        reps = N_KEYS // 8
        tau_t = _tile_rows(tau, reps)
        nsel = jnp.zeros_like(s1)
        rank = jnp.zeros_like(s2)
        for b in range(k):
            t2b = _tile_rows(t2[b], reps)
            nsel = nsel + jnp.where(s1 + t2b >= tau_t, 1.0, 0.0)
            rank = rank + jnp.where(t2b > s2, 1.0, 0.0)
        nsel_ref[h] = nsel
        rank_ref[h] = rank.astype(BF16)
        c_ref[h] = jnp.exp(s1 - _tile_rows(t1[0], reps)) / _tile_rows(zsum, reps)
        p_ref[h] = jnp.exp(s2 - _tile_rows(t2[0], reps)).astype(BF16)
        return carry

    lax.fori_loop(0, PEER_HEADS, head, 0)


def _peer_score(x1, wq_t, sk, tm):
    n = x1.shape[0]
    big = lambda dt: jax.ShapeDtypeStruct((PEER_HEADS, N_KEYS, n), dt)
    bspec = pl.BlockSpec((PEER_HEADS, N_KEYS, tm), lambda i: (0, 0, i))
    return pl.pallas_call(
        _peer_score_kernel,
        grid=(n // tm,),
        in_specs=[pl.BlockSpec((tm, D_MODEL), lambda i: (i, 0)),
                  pl.BlockSpec(wq_t.shape, lambda i: (0, 0)),
                  pl.BlockSpec(sk.shape, lambda i: (0, 0, 0))],
        out_specs=(pl.BlockSpec((D_MODEL, tm), lambda i: (0, i)), bspec, bspec, bspec, bspec),
        out_shape=(jax.ShapeDtypeStruct((D_MODEL, n), BF16), big(F32), big(F32), big(BF16), big(BF16)),
        compiler_params=_params("parallel"),
        name="peer_score",
    )(x1, wq_t, sk)


EXPERT_STEP = 1024
EXPERT_SUB = 256


def _peer_dense_kernel(alpha, x_ref, xt_ref, u_ref, vt_ref, nsel_ref, c_ref, rank_ref, p_ref,
                       lng_ref, lnb_ref, o_ref, acc_ref, act_ref, w_ref):
    e = pl.program_id(1)
    tm = x_ref.shape[0]
    nhalf = EXPERT_SUB // N_KEYS

    @pl.when(e == 0)
    def _():
        acc_ref[...] = jnp.zeros_like(acc_ref)

    def sel_row(ref, h, r, cs):
        tile = jnp.broadcast_to(ref[h, r:r + 1, cs], (16, 128)).astype(BF16)
        return jnp.concatenate([tile] * (N_KEYS // 16), axis=0)

    xt = xt_ref[...]
    for sub in range(EXPERT_STEP // EXPERT_SUB):
        rows = slice(sub * EXPERT_SUB, (sub + 1) * EXPERT_SUB)
        act_ref[...] = jax.nn.gelu(_dot(u_ref[rows, :], xt)).astype(BF16)
        for cc in range(tm // 128):
            cs = slice(cc * 128, (cc + 1) * 128)
            g = [jnp.zeros((N_KEYS, 128), BF16) for _ in range(nhalf)]
            for h in range(PEER_HEADS):
                rank = rank_ref[h, :, cs]
                p = p_ref[h, :, cs]
                for half in range(nhalf):
                    r = sub * nhalf + half
                    sel = rank < sel_row(nsel_ref, h, r, cs)
                    g[half] = g[half] + jnp.where(sel, p * sel_row(c_ref, h, r, cs), jnp.zeros((), BF16))
            for half in range(nhalf):
                er = slice(half * N_KEYS, (half + 1) * N_KEYS)
                w_ref[er, cs] = g[half] * act_ref[er, cs]
        acc_ref[...] += _dot(vt_ref[:, rows], w_ref[...])

    @pl.when(e == pl.num_programs(1) - 1)
    def _():
        y = acc_ref[...].T
        o_ref[...] = _layer_norm(alpha * x_ref[...] + y, lng_ref[...], lnb_ref[...])


def _peer_dense(alpha, x1, xt, u_b, vt_b, nsel, c, rank, p, lng, lnb, tm):
    n = x1.shape[0]
    ne = u_b.shape[0] // EXPERT_STEP
    big = pl.BlockSpec((PEER_HEADS, N_KEYS, tm), lambda i, e: (0, 0, i))
    rowsel = pl.BlockSpec((PEER_HEADS, EXPERT_STEP // N_KEYS, tm), lambda i, e: (0, e, i))
    return pl.pallas_call(
        functools.partial(_peer_dense_kernel, alpha),
        grid=(n // tm, ne),
        in_specs=[pl.BlockSpec((tm, D_MODEL), lambda i, e: (i, 0)),
                  pl.BlockSpec((D_MODEL, tm), lambda i, e: (0, i)),
                  pl.BlockSpec((EXPERT_STEP, D_MODEL), lambda i, e: (e, 0)),
                  pl.BlockSpec((D_MODEL, EXPERT_STEP), lambda i, e: (0, e)),
                  rowsel, rowsel, big, big,
                  pl.BlockSpec(lng.shape, lambda i, e: (0, 0)),
                  pl.BlockSpec(lnb.shape, lambda i, e: (0, 0))],
        out_specs=pl.BlockSpec((tm, D_MODEL), lambda i, e: (i, 0)),
        out_shape=jax.ShapeDtypeStruct(x1.shape, F32),
        scratch_shapes=[pltpu.VMEM((D_MODEL, tm), F32),
                        pltpu.VMEM((EXPERT_SUB, tm), BF16),
                        pltpu.VMEM((EXPERT_SUB, tm), BF16)],
        compiler_params=_params("parallel", "arbitrary"),
        name="peer_dense",
    )(x1, xt, u_b, vt_b, nsel, c, rank, p, lng, lnb)


def _peer(alpha, x1, wq_t, sk, u_b, vt_b, lng, lnb, tm):
    xt, nsel, c, rank, p = _peer_score(x1, wq_t, sk, tm)
    return _peer_dense(alpha, x1, xt, u_b, vt_b, nsel, c, rank, p, lng, lnb, tm)


def kernel(x_prompt, x_sample, mem_prompt, cache_sb_k, cache_sb_v, page_table, cache_mem_k, cache_mem_v,
           w_in, b_gate, gmlp_ln_g, gmlp_ln_b, w_spatial, b_spatial, sb_bias, w_mem_kv, w_br_gmlp, w_br_sb,
           w_br_mem, w_out, ln1_g, ln1_b, peer_wq, peer_subkeys, peer_u, peer_v, ln2_g, ln2_b):
    depth = w_in.shape[0]
    alpha = float((2 * depth) ** 0.25)
    batch, seq, d = x_prompt.shape
    nseq, ntok, _ = x_sample.shape
    assert seq % SB_BLOCK == 0 and (nseq * ntok) % 256 == 0 and CHUNK % ntok == 0

    xp = x_prompt.reshape(batch * seq, d)
    xs = x_sample.reshape(nseq * ntok, d)
    outs = [[] for _ in range(7)]
    row2 = lambda a: a.reshape(1, -1)
    for l in range(depth):
        w_a = w_in[l][:, :A_WIDTH].astype(BF16)
        w_gate = w_in[l][:, A_WIDTH:].astype(BF16)
        bg = b_gate[l].reshape(1, N_BRANCH * D_MODEL)
        lng, lnb = row2(gmlp_ln_g[l]), row2(gmlp_ln_b[l])
        ws_p = w_spatial[l].astype(BF16)
        bs_p = jnp.repeat(b_spatial[l].T, CHUNK, axis=1)
        reps = CHUNK // ntok
        eye = jnp.eye(reps, dtype=F32)
        ws_s = jnp.einsum('ab,gij->gaibj', eye, w_spatial[l][:, :ntok, :ntok]).reshape(G_GROUPS, CHUNK, CHUNK).astype(BF16)
        bs_s = jnp.tile(bs_p[:ntok], (reps, 1))
        w_bg, w_bs, w_bm, w_o = (w.astype(BF16) for w in (w_br_gmlp[l], w_br_sb[l], w_br_mem[l], w_out[l]))
        l1g, l1b, l2g, l2b = row2(ln1_g[l]), row2(ln1_b[l]), row2(ln2_g[l]), row2(ln2_b[l])
        wq_t = peer_wq[l].T.astype(BF16)
        sk = peer_subkeys[l].reshape(PEER_HEADS * 2, N_KEYS, PEER_DH).astype(BF16)
        u_b = peer_u[l].astype(BF16)
        vt_b = peer_v[l].T.astype(BF16)
        bias = sb_bias[l].astype(F32)

        gout, _, qb, k, kb, vv, vvb, qm = _proj(xp, w_a, lng, lnb, ws_p, bs_p)
        mk, mv = _mem_kv(mem_prompt.reshape(-1, d), w_mem_kv[l].astype(BF16))
        mk3, mv3 = mk.reshape(batch, N_MEM, MEM_WIDTH), mv.reshape(batch, N_MEM, MEM_WIDTH)
        sb = _sb_prompt(qb, kb, vvb, bias, batch, seq)
        mo = _mem_attn(qm, mk3, mv3, tm=512)
        x1 = _merge(alpha, xp, gout, sb, mo, w_gate, bg, w_bg, w_bs, w_bm, w_o, l1g, l1b)
        xp = _peer(alpha, x1, wq_t, sk, u_b, vt_b, l2g, l2b, tm=512)
        outs[0].append(k.reshape(batch, seq, SB_HEADS, SB_HD))
        outs[1].append(vv.reshape(batch, seq, SB_HEADS, SB_HD))
        outs[2].append(mk.reshape(batch, N_MEM, MEM_HEADS, MEM_HD))
        outs[3].append(mv.reshape(batch, N_MEM, MEM_HEADS, MEM_HD))

        gout, v, qb, k, kb, vv, vvb, qm = _proj(xs, w_a, lng, lnb, ws_s, bs_s)
        pad = lambda a: jnp.pad(a.reshape(nseq, ntok, SB_WIDTH), ((0, 0), (0, SB_CHUNK - ntok), (0, 0)))
        bias_rows = jnp.repeat(bias, ntok).reshape(SB_HEADS * ntok, 1)
        pages = lambda c: c.reshape(c.shape[0], c.shape[1], PAGE * SB_HEADS, SB_HD)
        sb = _sb_sample(qb.astype(F32).reshape(nseq, ntok, SB_WIDTH), pad(kb), pad(vvb), pages(cache_sb_k),
                        pages(cache_sb_v), l, page_table, bias_rows).reshape(nseq * ntok, SB_WIDTH).astype(BF16)
        mo = _mem_attn(qm.astype(F32), cache_mem_k[l].reshape(nseq, N_MEM, MEM_WIDTH),
                       cache_mem_v[l].reshape(nseq, N_MEM, MEM_WIDTH), tm=ntok).astype(BF16)
        x1 = _merge(alpha, xs, gout, sb, mo, w_gate, bg, w_bg, w_bs, w_bm, w_o, l1g, l1b)
        xs = _peer(alpha, x1, wq_t, sk, u_b, vt_b, l2g, l2b, tm=256)
        outs[4].append(k.reshape(nseq, ntok, SB_HEADS, SB_HD))
        outs[5].append(vv.reshape(nseq, ntok, SB_HEADS, SB_HD))
        outs[6].append(v.reshape(nseq, ntok, G_GROUPS, CHUNK))

    return (xp.reshape(batch, seq, d), xs.reshape(nseq, ntok, d)) + tuple(jnp.stack(o) for o in outs)
```

```python
import functools

import jax
import jax.numpy as jnp
from jax import lax
from jax.experimental import pallas as pl
from jax.experimental.pallas import tpu as pltpu

F32 = jnp.float32
BF16 = jnp.bfloat16

D_MODEL = 1024
G_WIDTH = 512
G_GROUPS = 4
CHUNK = 128
SB_HEADS = 8
SB_HD = 128
SB_WIDTH = 1024
MEM_HEADS = 4
MEM_HD = 128
MEM_WIDTH = 512
N_MEM = 256
N_BRANCH = 3
A_WIDTH = 2 * G_WIDTH + 3 * SB_WIDTH + MEM_WIDTH
PEER_HEADS = 8
PEER_DH = 128
N_KEYS = 128
PEER_TOPK = 16
PAGE = 128
LN_EPS = 1e-5
SB_SCALE = SB_HD ** -0.5
MEM_SCALE = MEM_HD ** -0.5

SB_CHUNK = 256
SB_BLOCK = 512
SB_HP = 2
PAGES_PER_STEP = 8
VMEM_LIMIT = 52 * 1024 * 1024


def _dot(a, b):
    return jnp.dot(a, b, preferred_element_type=F32)


def _dot_nt(a, b):
    return lax.dot_general(a, b, (((1,), (1,)), ((), ())), preferred_element_type=F32)


def _layer_norm(x, g, b):
    mu = jnp.mean(x, axis=-1, keepdims=True)
    xc = x - mu
    var = jnp.mean(xc * xc, axis=-1, keepdims=True)
    return xc * lax.rsqrt(var + LN_EPS) * g + b


def _params(*sem):
    return pltpu.CompilerParams(dimension_semantics=sem, vmem_limit_bytes=VMEM_LIMIT)


def _proj_kernel(x_ref, w_ref, lng_ref, lnb_ref, ws_ref, bs_ref,
                 gout_ref, v_ref, qb_ref, k_ref, kb_ref, vv_ref, vvb_ref, qm_ref):
    tm = x_ref.shape[0]
    xb = x_ref[...].astype(BF16)
    u = jax.nn.gelu(_dot(xb, w_ref[:, 0:G_WIDTH]))
    gv = jax.nn.gelu(_dot(xb, w_ref[:, G_WIDTH:2 * G_WIDTH]))
    v = _layer_norm(gv, lng_ref[...], lnb_ref[...])
    v_ref[...] = v
    vb = v.astype(BF16)
    row = lax.broadcasted_iota(jnp.int32, (CHUNK, CHUNK), 0)
    col = lax.broadcasted_iota(jnp.int32, (CHUNK, CHUNK), 1)
    tril = col <= row
    for g in range(G_GROUPS):
        wg = jnp.where(tril, ws_ref[g], jnp.zeros((), BF16))
        cols = slice(g * CHUNK, (g + 1) * CHUNK)
        for c in range(tm // CHUNK):
            rows = slice(c * CHUNK, (c + 1) * CHUNK)
            s = _dot(wg, vb[rows, cols]) + bs_ref[:, cols]
            gout_ref[rows, cols] = (u[rows, cols] * s).astype(BF16)
    o = 2 * G_WIDTH
    qb_ref[...] = (_dot(xb, w_ref[:, o:o + SB_WIDTH]) * SB_SCALE).astype(BF16)
    k = _dot(xb, w_ref[:, o + SB_WIDTH:o + 2 * SB_WIDTH])
    k_ref[...] = k
    kb_ref[...] = k.astype(BF16)
    vv = _dot(xb, w_ref[:, o + 2 * SB_WIDTH:o + 3 * SB_WIDTH])
    vv_ref[...] = vv
    vvb_ref[...] = vv.astype(BF16)
    o = o + 3 * SB_WIDTH
    qm_ref[...] = (_dot(xb, w_ref[:, o:o + MEM_WIDTH]) * MEM_SCALE).astype(BF16)


def _proj(x2, w_a, lng, lnb, ws, bs_full, tm=256):
    n = x2.shape[0]
    row = lambda w: pl.BlockSpec((tm, w), lambda i: (i, 0))
    full = lambda a: pl.BlockSpec(a.shape, lambda i: (0,) * a.ndim)
    out_shape = (
        jax.ShapeDtypeStruct((n, G_WIDTH), BF16),
        jax.ShapeDtypeStruct((n, G_WIDTH), F32),
        jax.ShapeDtypeStruct((n, SB_WIDTH), BF16),
        jax.ShapeDtypeStruct((n, SB_WIDTH), F32),
        jax.ShapeDtypeStruct((n, SB_WIDTH), BF16),
        jax.ShapeDtypeStruct((n, SB_WIDTH), F32),
        jax.ShapeDtypeStruct((n, SB_WIDTH), BF16),
        jax.ShapeDtypeStruct((n, MEM_WIDTH), BF16),
    )
    return pl.pallas_call(
        _proj_kernel,
        grid=(n // tm,),
        in_specs=[row(D_MODEL), full(w_a), full(lng), full(lnb), full(ws), full(bs_full)],
        out_specs=tuple(row(s.shape[1]) for s in out_shape),
        out_shape=out_shape,
        compiler_params=_params("parallel"),
        name="proj",
    )(x2, w_a, lng, lnb, ws, bs_full)


def _matmul2_kernel(x_ref, w_ref, a_ref, b_ref):
    y = _dot(x_ref[...].astype(BF16), w_ref[...])
    h = a_ref.shape[1]
    a_ref[...] = y[:, :h]
    b_ref[...] = y[:, h:]


def _mem_kv(mem2, w, tm=256):
    n = mem2.shape[0]
    h = w.shape[1] // 2
    return pl.pallas_call(
        _matmul2_kernel,
        grid=(n // tm,),
        in_specs=[pl.BlockSpec((tm, D_MODEL), lambda i: (i, 0)), pl.BlockSpec(w.shape, lambda i: (0, 0))],
        out_specs=(pl.BlockSpec((tm, h), lambda i: (i, 0)), pl.BlockSpec((tm, h), lambda i: (i, 0))),
        out_shape=(jax.ShapeDtypeStruct((n, h), F32), jax.ShapeDtypeStruct((n, h), F32)),
        compiler_params=_params("parallel"),
        name="mem_kv",
    )(mem2, w)


def _sb_block(q, kb, vb, r_mat, bias, carry, causal):
    c = SB_CHUNK
    z = _dot_nt(q, kb) + bias
    sp = jnp.maximum(z, 0.0) + jnp.log(1.0 + jnp.exp(-jnp.abs(z)))
    if causal is not None:
        sp = jnp.where(causal, sp, 0.0)
    spb = sp.astype(BF16)
    nchunk = kb.shape[0] // c
    local = [_dot(spb[:, j * c:(j + 1) * c], r_mat) for j in range(nchunk)]
    pieces = [None] * nchunk
    for j in range(nchunk - 1, -1, -1):
        pieces[j] = jnp.exp(z[:, j * c:(j + 1) * c] - local[j] - carry)
        carry = carry + local[j][:, 0:1]
    a = pieces[0] if nchunk == 1 else jnp.concatenate(pieces, axis=1)
    if causal is not None:
        a = jnp.where(causal, a, 0.0)
    return _dot(a.astype(BF16), vb), carry


def _sb_prompt_kernel(bias_ref, q_ref, k_ref, v_ref, r_ref, o_ref):
    hg = pl.program_id(1)
    i = pl.program_id(2)
    t = SB_BLOCK
    r_mat = r_ref[...]
    row = lax.broadcasted_iota(jnp.int32, (t, t), 0)
    col = lax.broadcasted_iota(jnp.int32, (t, t), 1)
    heads = [(slice(j * SB_HD, (j + 1) * SB_HD), bias_ref[hg * SB_HP + j]) for j in range(SB_HP)]

    def visit(start, state, causal):
        out = []
        for (cols, bias), (acc, carry) in zip(heads, state):
            d, carry = _sb_block(q_ref[:, cols], k_ref[pl.ds(start, t), cols], v_ref[pl.ds(start, t), cols],
                                 r_mat, bias, carry, causal)
            out.append((acc + d, carry))
        return tuple(out)

    zero = (jnp.zeros((t, SB_HD), F32), jnp.zeros((t, 1), F32))
    state = visit(pl.multiple_of(i * t, t), (zero,) * SB_HP, col < row)
    state = lax.fori_loop(0, i, lambda j, st: visit(pl.multiple_of((i - 1 - j) * t, t), st, None), state)
    for (cols, _), (acc, _) in zip(heads, state):
        o_ref[:, cols] = acc.astype(o_ref.dtype)


def _suffix_ones(n):
    j = lax.broadcasted_iota(jnp.int32, (n, n), 0)
    s = lax.broadcasted_iota(jnp.int32, (n, n), 1)
    return (j >= s).astype(BF16)


def _sb_prompt(qb, kb, vb, bias, batch, seq):
    t = SB_BLOCK
    nq = seq // t
    w = SB_HP * SB_HD
    return pl.pallas_call(
        _sb_prompt_kernel,
        grid_spec=pltpu.PrefetchScalarGridSpec(
            num_scalar_prefetch=1,
            grid=(batch, SB_HEADS // SB_HP, nq),
            in_specs=[
                pl.BlockSpec((t, w), lambda b, h, i, bias: (b * nq + i, h)),
                pl.BlockSpec((seq, w), lambda b, h, i, bias: (b, h)),
                pl.BlockSpec((seq, w), lambda b, h, i, bias: (b, h)),
                pl.BlockSpec((SB_CHUNK, SB_CHUNK), lambda b, h, i, bias: (0, 0)),
            ],
            out_specs=pl.BlockSpec((t, w), lambda b, h, i, bias: (b * nq + i, h)),
        ),
        out_shape=jax.ShapeDtypeStruct(qb.shape, BF16),
        compiler_params=_params("parallel", "parallel", "arbitrary"),
        name="sb_prompt",
    )(bias, qb, kb, vb, _suffix_ones(SB_CHUNK))


def _sb_sample_kernel(pt_ref, q_ref, kn_ref, vn_ref, *refs):
    npg = PAGES_PER_STEP
    k_refs = refs[:npg]
    v_refs = refs[npg:2 * npg]
    r_ref, bias_ref, o_ref, qbd_ref, acc_ref, carry_ref = refs[2 * npg:]
    s = pl.program_id(1)
    t = kn_ref.shape[1]
    nrow = q_ref.shape[1] * SB_HEADS
    ntok = q_ref.shape[1]
    r_mat = r_ref[...]
    bias = bias_ref[...]

    def visit(kb, vb, causal):
        d, c = _sb_block(qbd_ref[...], kb, vb, r_mat, bias, carry_ref[:, 0:1], causal)
        acc_ref[...] += d
        carry_ref[...] = jnp.broadcast_to(c, carry_ref.shape)

    @pl.when(s == 0)
    def _():
        qrep = jnp.concatenate([q_ref[0]] * SB_HEADS, axis=0)
        rh = lax.broadcasted_iota(jnp.int32, qrep.shape, 0) // ntok
        ch = lax.broadcasted_iota(jnp.int32, qrep.shape, 1) // SB_HD
        qbd_ref[...] = jnp.where(rh == ch, qrep, 0.0).astype(BF16)
        acc_ref[...] = jnp.zeros_like(acc_ref)
        carry_ref[...] = jnp.zeros_like(carry_ref)
        tq = lax.broadcasted_iota(jnp.int32, (nrow, t), 0) % ntok
        tk = lax.broadcasted_iota(jnp.int32, (nrow, t), 1)
        visit(kn_ref[0], vn_ref[0], tk < tq)

    def rows(page_ref):
        return jnp.concatenate([page_ref[pl.ds(h, PAGE, stride=SB_HEADS), :] for h in range(SB_HEADS)],
                               axis=1).astype(BF16)

    visit(jnp.concatenate([rows(r) for r in k_refs], axis=0), jnp.concatenate([rows(r) for r in v_refs], axis=0), None)

    @pl.when(s == pl.num_programs(1) - 1)
    def _():
        for h in range(SB_HEADS):
            cols = slice(h * SB_HD, (h + 1) * SB_HD)
            o_ref[0, :, cols] = acc_ref[h * ntok:(h + 1) * ntok, cols].astype(o_ref.dtype)


def _sb_sample(q3, kn_pad, vn_pad, cache_k, cache_v, layer, page_table, bias_rows):
    nseq, ntok, _ = q3.shape
    npages = page_table.shape[1]
    npg = PAGES_PER_STEP
    nsteps = npages // npg
    t = kn_pad.shape[1]
    nrow = ntok * SB_HEADS

    def page_spec(j):
        return pl.BlockSpec((None, None, PAGE * SB_HEADS, SB_HD),
                            lambda b, s, pt: (layer, pt[b, (nsteps - 1 - s) * npg + j], 0, 0))

    seq_spec = lambda r: pl.BlockSpec((1, r, SB_WIDTH), lambda b, s, pt: (b, 0, 0))
    return pl.pallas_call(
        _sb_sample_kernel,
        grid_spec=pltpu.PrefetchScalarGridSpec(
            num_scalar_prefetch=1,
            grid=(nseq, nsteps),
            in_specs=[seq_spec(ntok), seq_spec(t), seq_spec(t)]
            + [page_spec(j) for j in range(npg)] + [page_spec(j) for j in range(npg)]
            + [pl.BlockSpec((SB_CHUNK, SB_CHUNK), lambda b, s, pt: (0, 0)),
               pl.BlockSpec((nrow, 1), lambda b, s, pt: (0, 0))],
            out_specs=seq_spec(ntok),
            scratch_shapes=[pltpu.VMEM((nrow, SB_WIDTH), BF16),
                            pltpu.VMEM((nrow, SB_WIDTH), F32),
                            pltpu.VMEM((nrow, 128), F32)],
        ),
        out_shape=jax.ShapeDtypeStruct(q3.shape, F32),
        compiler_params=_params("parallel", "arbitrary"),
        name="sb_sample",
    )(page_table, q3, kn_pad, vn_pad, *([cache_k] * npg), *([cache_v] * npg), _suffix_ones(SB_CHUNK), bias_rows)


def _mem_attn_kernel(q_ref, k_ref, v_ref, o_ref):
    for h in range(MEM_HEADS):
        cols = slice(h * MEM_HD, (h + 1) * MEM_HD)
        s = _dot_nt(q_ref[:, cols].astype(BF16), k_ref[:, cols].astype(BF16))
        e = jnp.exp(s - jnp.max(s, axis=-1, keepdims=True))
        p = e / jnp.sum(e, axis=-1, keepdims=True)
        o_ref[:, cols] = _dot(p.astype(BF16), v_ref[:, cols].astype(BF16)).astype(o_ref.dtype)


def _mem_attn(qm, mk3, mv3, tm):
    n = qm.shape[0]
    nb = mk3.shape[0]
    nt = n // (nb * tm)
    kv = pl.BlockSpec((None, N_MEM, MEM_WIDTH), lambda b, i: (b, 0, 0))
    return pl.pallas_call(
        _mem_attn_kernel,
        grid=(nb, nt),
        in_specs=[pl.BlockSpec((tm, MEM_WIDTH), lambda b, i: (b * nt + i, 0)), kv, kv],
        out_specs=pl.BlockSpec((tm, MEM_WIDTH), lambda b, i: (b * nt + i, 0)),
        out_shape=jax.ShapeDtypeStruct(qm.shape, qm.dtype),
        compiler_params=_params("parallel", "parallel"),
        name="mem_attn",
    )(qm, mk3, mv3)


def _merge_kernel(alpha, x_ref, g_ref, sb_ref, m_ref, wg_ref, bg_ref, wbg_ref, wbs_ref, wbm_ref, wo_ref,
                  lng_ref, lnb_ref, o_ref):
    x = x_ref[...]
    gates = jax.nn.sigmoid(_dot(x.astype(BF16), wg_ref[...]) + bg_ref[...])
    d = D_MODEL
    z = (gates[:, 0:d] * _dot(g_ref[...], wbg_ref[...])
         + gates[:, d:2 * d] * _dot(sb_ref[...], wbs_ref[...])
         + gates[:, 2 * d:3 * d] * _dot(m_ref[...], wbm_ref[...]))
    mix = _dot(z.astype(BF16), wo_ref[...])
    o_ref[...] = _layer_norm(alpha * x + mix, lng_ref[...], lnb_ref[...])


def _merge(alpha, x2, gout, sb, mo, w_gate, b_gate, w_bg, w_bs, w_bm, w_o, lng, lnb, tm=256):
    n = x2.shape[0]
    row = lambda a: pl.BlockSpec((tm, a.shape[1]), lambda i: (i, 0))
    full = lambda a: pl.BlockSpec(a.shape, lambda i: (0,) * a.ndim)
    ws = (w_gate, b_gate, w_bg, w_bs, w_bm, w_o, lng, lnb)
    return pl.pallas_call(
        functools.partial(_merge_kernel, alpha),
        grid=(n // tm,),
        in_specs=[row(x2), row(gout), row(sb), row(mo)] + [full(a) for a in ws],
        out_specs=row(x2),
        out_shape=jax.ShapeDtypeStruct(x2.shape, F32),
        compiler_params=_params("parallel"),
        name="merge",
    )(x2, gout, sb, mo, *ws)


def _cmpx(v, i, j):
    hi = jnp.maximum(v[i], v[j])
    lo = jnp.minimum(v[i], v[j])
    v[i], v[j] = hi, lo


def _sort16_desc(v):
    v = list(v)
    n = len(v)
    k = 2
    while k <= n:
        j = k // 2
        while j >= 1:
            for i in range(n):
                l = i ^ j
                if l > i:
                    if (i & k) == 0:
                        _cmpx(v, i, l)
                    else:
                        _cmpx(v, l, i)
            j //= 2
        k *= 2
    return v


def _merge_top16(a, b):
    n = len(a)
    c = []
    for i in range(n):
        j = n - 1 - i
        c.append(jnp.maximum(a[i], b[j]) if j < len(b) else a[i])
    j = n // 2
    while j >= 1:
        for i in range(n):
            if (i & j) == 0:
                _cmpx(c, i, i + j)
        j //= 2
    return c


def _top16_rows(s):
    g = _sort16_desc([s[8 * i:8 * i + 8, :] for i in range(N_KEYS // 8)])
    for shift in (4, 2, 1):
        g = _merge_top16(g, [pltpu.roll(x, shift, 0) for x in g])
    return g


def _tile_rows(a, reps):
    return jnp.concatenate([a] * reps, axis=0)


_PK_T1 = 0
_PK_T2 = PEER_TOPK
_PK_CNT = 2 * PEER_TOPK
_PK_INVZ = 3 * PEER_TOPK + 1
_PK_ROWS = 3 * PEER_TOPK + 2


def _peer_score_kernel(x_ref, wq_ref, sk_ref, xt_ref, nsel_ref, c_ref, rank_ref, p_ref, s1_s, s2_s, pk_ref):
    tm = x_ref.shape[0]
    xt = x_ref[...].T.astype(BF16)
    xt_ref[...] = xt
    k = PEER_TOPK
    sub = lax.broadcasted_iota(jnp.int32, (8, tm), 0)
    pk_ref[0:2 * k] = jnp.zeros((2 * k, 8, tm), F32)

    def scores(h, carry):
        qt = _dot(wq_ref[pl.ds(pl.multiple_of(h * 2 * PEER_DH, 2 * PEER_DH), 2 * PEER_DH), :], xt)
        s1 = _dot(sk_ref[2 * h], qt[:PEER_DH].astype(BF16))
        s2 = _dot(sk_ref[2 * h + 1], qt[PEER_DH:].astype(BF16))
        s1_s[h] = s1
        s2_s[h] = s2
        mine = sub == h
        for base, s in ((_PK_T1, s1), (_PK_T2, s2)):
            for a, t in enumerate(_top16_rows(s)):
                pk_ref[base + a] = jnp.where(mine, t, pk_ref[base + a])
        return carry

    lax.fori_loop(0, PEER_HEADS, scores, 0)

    t1 = [pk_ref[_PK_T1 + a] for a in range(k)]
    t2 = [pk_ref[_PK_T2 + a] for a in range(k)]
    top = [t1[0] + t2[b] for b in range(k)]
    for a in range(1, k):
        top = _merge_top16(top, [t1[a] + t2[b] for b in range(k // (a + 1))])
    tau = top[k - 1]
    m = t1[0] + t2[0]
    zsum = jnp.zeros_like(m)
    for a in range(k):
        cnt = jnp.zeros_like(m)
        for b in range(k // (a + 1)):
            c = t1[a] + t2[b]
            sel = c >= tau
            zsum = zsum + jnp.where(sel, jnp.exp(c - m), 0.0)
            cnt = cnt + jnp.where(sel, 1.0, 0.0)
        pk_ref[_PK_CNT + a] = cnt
    pk_ref[_PK_CNT + k] = jnp.zeros_like(m)
    pk_ref[_PK_INVZ] = 1.0 / zsum

    for h in range(PEER_HEADS):
        def row(i, cs):
            return jnp.broadcast_to(pk_ref[i, h:h + 1, cs], (N_KEYS, 128))

        for cc in range(tm // 128):
            cs = slice(cc * 128, (cc + 1) * 128)
            s2 = s2_s[h, :, cs]
            rank = jnp.zeros_like(s2)
            for a in range(k):
                rank = jnp.where(row(_PK_T2 + a, cs) > s2, float(a + 1), rank)
            rank_ref[h, :, cs] = rank.astype(BF16)
            p_ref[h, :, cs] = jnp.exp(s2 - row(_PK_T2, cs)).astype(BF16)
            s1 = s1_s[h, :, cs]
            nsel = row(_PK_CNT, cs)
            for a in range(k):
                nsel = jnp.where(row(_PK_T1 + a, cs) > s1, row(_PK_CNT + a + 1, cs), nsel)
            nsel_ref[h, :, cs] = nsel
            c_ref[h, :, cs] = jnp.exp(s1 - row(_PK_T1, cs)) * row(_PK_INVZ, cs)


def _peer_score(x1, wq_t, sk, tm):
    n = x1.shape[0]
    big = lambda dt: jax.ShapeDtypeStruct((PEER_HEADS, N_KEYS, n), dt)
    bspec = pl.BlockSpec((PEER_HEADS, N_KEYS, tm), lambda i: (0, 0, i))
    return pl.pallas_call(
        _peer_score_kernel,
        grid=(n // tm,),
        in_specs=[pl.BlockSpec((tm, D_MODEL), lambda i: (i, 0)),
                  pl.BlockSpec(wq_t.shape, lambda i: (0, 0)),
                  pl.BlockSpec(sk.shape, lambda i: (0, 0, 0))],
        out_specs=(pl.BlockSpec((D_MODEL, tm), lambda i: (0, i)), bspec, bspec, bspec, bspec),
        out_shape=(jax.ShapeDtypeStruct((D_MODEL, n), BF16), big(F32), big(F32), big(BF16), big(BF16)),
        scratch_shapes=[pltpu.VMEM((PEER_HEADS, N_KEYS, tm), F32),
                        pltpu.VMEM((PEER_HEADS, N_KEYS, tm), F32),
                        pltpu.VMEM((_PK_ROWS, 8, tm), F32)],
        compiler_params=_params("parallel"),
        name="peer_score",
    )(x1, wq_t, sk)


EXPERT_STEP = 1024
EXPERT_SUB = 256


def _peer_dense_kernel(alpha, x_ref, xt_ref, u_ref, vt_ref, nsel_ref, c_ref, rank_ref, p_ref,
                       lng_ref, lnb_ref, o_ref, acc_ref, act_ref, w_ref, rank_s, p_s):
    e = pl.program_id(1)
    tm = x_ref.shape[0]
    nhalf = EXPERT_SUB // N_KEYS
    nsub = EXPERT_STEP // EXPERT_SUB

    @pl.when(e == 0)
    def _():
        acc_ref[...] = jnp.zeros_like(acc_ref)
        rank_s[...] = rank_ref[...]
        p_s[...] = p_ref[...]

    def sel_row(ref, h, r, cs):
        tile = jnp.broadcast_to(ref[h, r:r + 1, cs], (16, 128)).astype(BF16)
        return jnp.concatenate([tile] * (N_KEYS // 16), axis=0)

    xt = xt_ref[...]
    for sub in range(nsub):
        rows = slice(sub * EXPERT_SUB, (sub + 1) * EXPERT_SUB)
        act_ref[rows, :] = jax.nn.gelu(_dot(u_ref[rows, :], xt).astype(BF16))
    for sub in range(nsub):
        for cc in range(tm // 128):
            cs = slice(cc * 128, (cc + 1) * 128)
            g = [jnp.zeros((N_KEYS, 128), BF16) for _ in range(nhalf)]
            for h in range(PEER_HEADS):
                rank = rank_s[h, :, cs]
                p = p_s[h, :, cs]
                for half in range(nhalf):
                    r = sub * nhalf + half
                    wsel = jnp.minimum(jnp.maximum(sel_row(nsel_ref, h, r, cs) - rank, 0), sel_row(c_ref, h, r, cs))
                    g[half] = g[half] + p * wsel
            for half in range(nhalf):
                er = slice(sub * EXPERT_SUB + half * N_KEYS, sub * EXPERT_SUB + (half + 1) * N_KEYS)
                w_ref[er, cs] = g[half] * act_ref[er, cs]
    acc_ref[...] += _dot(vt_ref[...], w_ref[...])

    @pl.when(e == pl.num_programs(1) - 1)
    def _():
        y = acc_ref[...].T
        o_ref[...] = _layer_norm(alpha * x_ref[...] + y, lng_ref[...], lnb_ref[...])


def _peer_dense(alpha, x1, xt, u_b, vt_b, nsel, c, rank, p, lng, lnb, tm):
    n = x1.shape[0]
    ne = u_b.shape[0] // EXPERT_STEP
    big = pl.BlockSpec((PEER_HEADS, N_KEYS, tm), lambda i, e: (0, 0, i))
    rowsel = pl.BlockSpec((PEER_HEADS, EXPERT_STEP // N_KEYS, tm), lambda i, e: (0, e, i))
    return pl.pallas_call(
        functools.partial(_peer_dense_kernel, alpha),
        grid=(n // tm, ne),
        in_specs=[pl.BlockSpec((tm, D_MODEL), lambda i, e: (i, 0)),
                  pl.BlockSpec((D_MODEL, tm), lambda i, e: (0, i)),
                  pl.BlockSpec((EXPERT_STEP, D_MODEL), lambda i, e: (e, 0)),
                  pl.BlockSpec((D_MODEL, EXPERT_STEP), lambda i, e: (0, e)),
                  rowsel, rowsel, big, big,
                  pl.BlockSpec(lng.shape, lambda i, e: (0, 0)),
                  pl.BlockSpec(lnb.shape, lambda i, e: (0, 0))],
        out_specs=pl.BlockSpec((tm, D_MODEL), lambda i, e: (i, 0)),
        out_shape=jax.ShapeDtypeStruct(x1.shape, F32),
        scratch_shapes=[pltpu.VMEM((D_MODEL, tm), F32),
                        pltpu.VMEM((EXPERT_STEP, tm), BF16),
                        pltpu.VMEM((EXPERT_STEP, tm), BF16),
                        pltpu.VMEM((PEER_HEADS, N_KEYS, tm), BF16),
                        pltpu.VMEM((PEER_HEADS, N_KEYS, tm), BF16)],
        compiler_params=_params("parallel", "arbitrary"),
        name="peer_dense",
    )(x1, xt, u_b, vt_b, nsel, c, rank, p, lng, lnb)


def _peer(alpha, x1, wq_t, sk, u_b, vt_b, lng, lnb, tm):
    xt, nsel, c, rank, p = _peer_score(x1, wq_t, sk, tm)
    return _peer_dense(alpha, x1, xt, u_b, vt_b, nsel, c, rank, p, lng, lnb, tm)


def kernel(x_prompt, x_sample, mem_prompt, cache_sb_k, cache_sb_v, page_table, cache_mem_k, cache_mem_v,
           w_in, b_gate, gmlp_ln_g, gmlp_ln_b, w_spatial, b_spatial, sb_bias, w_mem_kv, w_br_gmlp, w_br_sb,
           w_br_mem, w_out, ln1_g, ln1_b, peer_wq, peer_subkeys, peer_u, peer_v, ln2_g, ln2_b):
    depth = w_in.shape[0]
    alpha = float((2 * depth) ** 0.25)
    batch, seq, d = x_prompt.shape
    nseq, ntok, _ = x_sample.shape
    assert seq % SB_BLOCK == 0 and (nseq * ntok) % 256 == 0 and CHUNK % ntok == 0

    xp = x_prompt.reshape(batch * seq, d)
    xs = x_sample.reshape(nseq * ntok, d)
    outs = [[] for _ in range(7)]
    row2 = lambda a: a.reshape(1, -1)
    for l in range(depth):
        w_a = w_in[l][:, :A_WIDTH].astype(BF16)
        w_gate = w_in[l][:, A_WIDTH:].astype(BF16)
        bg = b_gate[l].reshape(1, N_BRANCH * D_MODEL)
        lng, lnb = row2(gmlp_ln_g[l]), row2(gmlp_ln_b[l])
        ws_p = w_spatial[l].astype(BF16)
        bs_p = jnp.repeat(b_spatial[l].T, CHUNK, axis=1)
        reps = CHUNK // ntok
        eye = jnp.eye(reps, dtype=F32)
        ws_s = jnp.einsum('ab,gij->gaibj', eye, w_spatial[l][:, :ntok, :ntok]).reshape(G_GROUPS, CHUNK, CHUNK).astype(BF16)
        bs_s = jnp.tile(bs_p[:ntok], (reps, 1))
        w_bg, w_bs, w_bm, w_o = (w.astype(BF16) for w in (w_br_gmlp[l], w_br_sb[l], w_br_mem[l], w_out[l]))
        l1g, l1b, l2g, l2b = row2(ln1_g[l]), row2(ln1_b[l]), row2(ln2_g[l]), row2(ln2_b[l])
        wq_t = peer_wq[l].T.astype(BF16)
        sk = peer_subkeys[l].reshape(PEER_HEADS * 2, N_KEYS, PEER_DH).astype(BF16)
        u_b = peer_u[l].astype(BF16)
        vt_b = peer_v[l].T.astype(BF16)
        bias = sb_bias[l].astype(F32)

        gout, _, qb, k, kb, vv, vvb, qm = _proj(xp, w_a, lng, lnb, ws_p, bs_p)
        mk, mv = _mem_kv(mem_prompt.reshape(-1, d), w_mem_kv[l].astype(BF16))
        mk3, mv3 = mk.reshape(batch, N_MEM, MEM_WIDTH), mv.reshape(batch, N_MEM, MEM_WIDTH)
        sb = _sb_prompt(qb, kb, vvb, bias, batch, seq)
        mo = _mem_attn(qm, mk3, mv3, tm=512)
        x1 = _merge(alpha, xp, gout, sb, mo, w_gate, bg, w_bg, w_bs, w_bm, w_o, l1g, l1b)
        xp = _peer(alpha, x1, wq_t, sk, u_b, vt_b, l2g, l2b, tm=512)
        outs[0].append(k.reshape(batch, seq, SB_HEADS, SB_HD))
        outs[1].append(vv.reshape(batch, seq, SB_HEADS, SB_HD))
        outs[2].append(mk.reshape(batch, N_MEM, MEM_HEADS, MEM_HD))
        outs[3].append(mv.reshape(batch, N_MEM, MEM_HEADS, MEM_HD))

        gout, v, qb, k, kb, vv, vvb, qm = _proj(xs, w_a, lng, lnb, ws_s, bs_s)
        pad = lambda a: jnp.pad(a.reshape(nseq, ntok, SB_WIDTH), ((0, 0), (0, SB_CHUNK - ntok), (0, 0)))
        bias_rows = jnp.repeat(bias, ntok).reshape(SB_HEADS * ntok, 1)
        pages = lambda c: c.reshape(c.shape[0], c.shape[1], PAGE * SB_HEADS, SB_HD)
        sb = _sb_sample(qb.astype(F32).reshape(nseq, ntok, SB_WIDTH), pad(kb), pad(vvb), pages(cache_sb_k),
                        pages(cache_sb_v), l, page_table, bias_rows).reshape(nseq * ntok, SB_WIDTH).astype(BF16)
        mo = _mem_attn(qm.astype(F32), cache_mem_k[l].reshape(nseq, N_MEM, MEM_WIDTH),
                       cache_mem_v[l].reshape(nseq, N_MEM, MEM_WIDTH), tm=ntok).astype(BF16)
        x1 = _merge(alpha, xs, gout, sb, mo, w_gate, bg, w_bg, w_bs, w_bm, w_o, l1g, l1b)
        xs = _peer(alpha, x1, wq_t, sk, u_b, vt_b, l2g, l2b, tm=256)
        outs[4].append(k.reshape(nseq, ntok, SB_HEADS, SB_HD))
        outs[5].append(vv.reshape(nseq, ntok, SB_HEADS, SB_HD))
        outs[6].append(v.reshape(nseq, ntok, G_GROUPS, CHUNK))

    return (xp.reshape(batch, seq, d), xs.reshape(nseq, ntok, d)) + tuple(jnp.stack(o) for o in outs)
```

```python
import functools

import jax
import jax.numpy as jnp
from jax import lax
from jax.experimental import pallas as pl
from jax.experimental.pallas import tpu as pltpu

F32 = jnp.float32
BF16 = jnp.bfloat16

D_MODEL = 1024
G_WIDTH = 512
G_GROUPS = 4
CHUNK = 128
SB_HEADS = 8
SB_HD = 128
SB_WIDTH = 1024
MEM_HEADS = 4
MEM_HD = 128
MEM_WIDTH = 512
N_MEM = 256
N_BRANCH = 3
A_WIDTH = 2 * G_WIDTH + 3 * SB_WIDTH + MEM_WIDTH
PEER_HEADS = 8
PEER_DH = 128
N_KEYS = 128
PEER_TOPK = 16
PAGE = 128
LN_EPS = 1e-5
SB_SCALE = SB_HD ** -0.5
MEM_SCALE = MEM_HD ** -0.5

SB_CHUNK = 256
SB_BLOCK = 512
SB_HP = 2
PAGES_PER_STEP = 8
VMEM_LIMIT = 52 * 1024 * 1024


def _dot(a, b):
    return jnp.dot(a, b, preferred_element_type=F32)


def _dot_nt(a, b):
    return lax.dot_general(a, b, (((1,), (1,)), ((), ())), preferred_element_type=F32)


def _layer_norm(x, g, b):
    mu = jnp.mean(x, axis=-1, keepdims=True)
    xc = x - mu
    var = jnp.mean(xc * xc, axis=-1, keepdims=True)
    return xc * lax.rsqrt(var + LN_EPS) * g + b


def _params(*sem):
    return pltpu.CompilerParams(dimension_semantics=sem, vmem_limit_bytes=VMEM_LIMIT)


def _proj_kernel(x_ref, w_ref, lng_ref, lnb_ref, ws_ref, bs_ref,
                 gout_ref, v_ref, qb_ref, k_ref, kb_ref, vv_ref, vvb_ref, qm_ref):
    tm = x_ref.shape[0]
    xb = x_ref[...].astype(BF16)
    u = jax.nn.gelu(_dot(xb, w_ref[:, 0:G_WIDTH]))
    gv = jax.nn.gelu(_dot(xb, w_ref[:, G_WIDTH:2 * G_WIDTH]))
    v = _layer_norm(gv, lng_ref[...], lnb_ref[...])
    v_ref[...] = v
    vb = v.astype(BF16)
    row = lax.broadcasted_iota(jnp.int32, (CHUNK, CHUNK), 0)
    col = lax.broadcasted_iota(jnp.int32, (CHUNK, CHUNK), 1)
    tril = col <= row
    for g in range(G_GROUPS):
        wg = jnp.where(tril, ws_ref[g], jnp.zeros((), BF16))
        cols = slice(g * CHUNK, (g + 1) * CHUNK)
        for c in range(tm // CHUNK):
            rows = slice(c * CHUNK, (c + 1) * CHUNK)
            s = _dot(wg, vb[rows, cols]) + bs_ref[:, cols]
            gout_ref[rows, cols] = (u[rows, cols] * s).astype(BF16)
    o = 2 * G_WIDTH
    qb_ref[...] = (_dot(xb, w_ref[:, o:o + SB_WIDTH]) * SB_SCALE).astype(BF16)
    k = _dot(xb, w_ref[:, o + SB_WIDTH:o + 2 * SB_WIDTH])
    k_ref[...] = k
    kb_ref[...] = k.astype(BF16)
    vv = _dot(xb, w_ref[:, o + 2 * SB_WIDTH:o + 3 * SB_WIDTH])
    vv_ref[...] = vv
    vvb_ref[...] = vv.astype(BF16)
    o = o + 3 * SB_WIDTH
    qm_ref[...] = (_dot(xb, w_ref[:, o:o + MEM_WIDTH]) * MEM_SCALE).astype(BF16)


def _proj(x2, w_a, lng, lnb, ws, bs_full, tm=256):
    n = x2.shape[0]
    row = lambda w: pl.BlockSpec((tm, w), lambda i: (i, 0))
    full = lambda a: pl.BlockSpec(a.shape, lambda i: (0,) * a.ndim)
    out_shape = (
        jax.ShapeDtypeStruct((n, G_WIDTH), BF16),
        jax.ShapeDtypeStruct((n, G_WIDTH), F32),
        jax.ShapeDtypeStruct((n, SB_WIDTH), BF16),
        jax.ShapeDtypeStruct((n, SB_WIDTH), F32),
        jax.ShapeDtypeStruct((n, SB_WIDTH), BF16),
        jax.ShapeDtypeStruct((n, SB_WIDTH), F32),
        jax.ShapeDtypeStruct((n, SB_WIDTH), BF16),
        jax.ShapeDtypeStruct((n, MEM_WIDTH), BF16),
    )
    return pl.pallas_call(
        _proj_kernel,
        grid=(n // tm,),
        in_specs=[row(D_MODEL), full(w_a), full(lng), full(lnb), full(ws), full(bs_full)],
        out_specs=tuple(row(s.shape[1]) for s in out_shape),
        out_shape=out_shape,
        compiler_params=_params("parallel"),
        name="proj",
    )(x2, w_a, lng, lnb, ws, bs_full)


def _matmul2_kernel(x_ref, w_ref, a_ref, b_ref):
    y = _dot(x_ref[...].astype(BF16), w_ref[...])
    h = a_ref.shape[1]
    a_ref[...] = y[:, :h]
    b_ref[...] = y[:, h:]


def _mem_kv(mem2, w, tm=256):
    n = mem2.shape[0]
    h = w.shape[1] // 2
    return pl.pallas_call(
        _matmul2_kernel,
        grid=(n // tm,),
        in_specs=[pl.BlockSpec((tm, D_MODEL), lambda i: (i, 0)), pl.BlockSpec(w.shape, lambda i: (0, 0))],
        out_specs=(pl.BlockSpec((tm, h), lambda i: (i, 0)), pl.BlockSpec((tm, h), lambda i: (i, 0))),
        out_shape=(jax.ShapeDtypeStruct((n, h), F32), jax.ShapeDtypeStruct((n, h), F32)),
        compiler_params=_params("parallel"),
        name="mem_kv",
    )(mem2, w)


def _sb_block(q, kb, vb, r_mat, bias, carry, causal):
    c = SB_CHUNK
    z = _dot_nt(q, kb) + bias
    sp = jnp.maximum(z, 0.0) + jnp.log(1.0 + jnp.exp(-jnp.abs(z)))
    if causal is not None:
        sp = jnp.where(causal, sp, 0.0)
    spb = sp.astype(BF16)
    nchunk = kb.shape[0] // c
    local = [_dot(spb[:, j * c:(j + 1) * c], r_mat) for j in range(nchunk)]
    pieces = [None] * nchunk
    for j in range(nchunk - 1, -1, -1):
        pieces[j] = jnp.exp(z[:, j * c:(j + 1) * c] - local[j] - carry)
        carry = carry + local[j][:, 0:1]
    a = pieces[0] if nchunk == 1 else jnp.concatenate(pieces, axis=1)
    if causal is not None:
        a = jnp.where(causal, a, 0.0)
    return _dot(a.astype(BF16), vb), carry


def _sb_prompt_kernel(bias_ref, q_ref, k_ref, v_ref, r_ref, o_ref):
    hg = pl.program_id(1)
    i = pl.program_id(2)
    t = SB_BLOCK
    r_mat = r_ref[...]
    row = lax.broadcasted_iota(jnp.int32, (t, t), 0)
    col = lax.broadcasted_iota(jnp.int32, (t, t), 1)
    heads = [(slice(j * SB_HD, (j + 1) * SB_HD), bias_ref[hg * SB_HP + j]) for j in range(SB_HP)]

    def visit(start, state, causal):
        out = []
        for (cols, bias), (acc, carry) in zip(heads, state):
            d, carry = _sb_block(q_ref[:, cols], k_ref[pl.ds(start, t), cols], v_ref[pl.ds(start, t), cols],
                                 r_mat, bias, carry, causal)
            out.append((acc + d, carry))
        return tuple(out)

    zero = (jnp.zeros((t, SB_HD), F32), jnp.zeros((t, 1), F32))
    state = visit(pl.multiple_of(i * t, t), (zero,) * SB_HP, col < row)
    state = lax.fori_loop(0, i, lambda j, st: visit(pl.multiple_of((i - 1 - j) * t, t), st, None), state)
    for (cols, _), (acc, _) in zip(heads, state):
        o_ref[:, cols] = acc.astype(o_ref.dtype)


def _suffix_ones(n):
    j = lax.broadcasted_iota(jnp.int32, (n, n), 0)
    s = lax.broadcasted_iota(jnp.int32, (n, n), 1)
    return (j >= s).astype(BF16)


def _sb_prompt(qb, kb, vb, bias, batch, seq):
    t = SB_BLOCK
    nq = seq // t
    w = SB_HP * SB_HD
    return pl.pallas_call(
        _sb_prompt_kernel,
        grid_spec=pltpu.PrefetchScalarGridSpec(
            num_scalar_prefetch=1,
            grid=(batch, SB_HEADS // SB_HP, nq),
            in_specs=[
                pl.BlockSpec((t, w), lambda b, h, i, bias: (b * nq + i, h)),
                pl.BlockSpec((seq, w), lambda b, h, i, bias: (b, h)),
                pl.BlockSpec((seq, w), lambda b, h, i, bias: (b, h)),
                pl.BlockSpec((SB_CHUNK, SB_CHUNK), lambda b, h, i, bias: (0, 0)),
            ],
            out_specs=pl.BlockSpec((t, w), lambda b, h, i, bias: (b * nq + i, h)),
        ),
        out_shape=jax.ShapeDtypeStruct(qb.shape, BF16),
        compiler_params=_params("parallel", "parallel", "arbitrary"),
        name="sb_prompt",
    )(bias, qb, kb, vb, _suffix_ones(SB_CHUNK))


def _sb_sample_kernel(pt_ref, q_ref, kn_ref, vn_ref, *refs):
    npg = PAGES_PER_STEP
    k_refs = refs[:npg]
    v_refs = refs[npg:2 * npg]
    r_ref, bias_ref, o_ref, qbd_ref, acc_ref, carry_ref = refs[2 * npg:]
    s = pl.program_id(1)
    t = kn_ref.shape[1]
    nrow = q_ref.shape[1] * SB_HEADS
    ntok = q_ref.shape[1]
    r_mat = r_ref[...]
    bias = bias_ref[...]

    def visit(kb, vb, causal):
        d, c = _sb_block(qbd_ref[...], kb, vb, r_mat, bias, carry_ref[:, 0:1], causal)
        acc_ref[...] += d
        carry_ref[...] = jnp.broadcast_to(c, carry_ref.shape)

    @pl.when(s == 0)
    def _():
        qrep = jnp.concatenate([q_ref[0]] * SB_HEADS, axis=0)
        rh = lax.broadcasted_iota(jnp.int32, qrep.shape, 0) // ntok
        ch = lax.broadcasted_iota(jnp.int32, qrep.shape, 1) // SB_HD
        qbd_ref[...] = jnp.where(rh == ch, qrep, 0.0).astype(BF16)
        acc_ref[...] = jnp.zeros_like(acc_ref)
        carry_ref[...] = jnp.zeros_like(carry_ref)
        tq = lax.broadcasted_iota(jnp.int32, (nrow, t), 0) % ntok
        tk = lax.broadcasted_iota(jnp.int32, (nrow, t), 1)
        visit(kn_ref[0], vn_ref[0], tk < tq)

    def rows(page_ref):
        return jnp.concatenate([page_ref[pl.ds(h, PAGE, stride=SB_HEADS), :] for h in range(SB_HEADS)],
                               axis=1).astype(BF16)

    visit(jnp.concatenate([rows(r) for r in k_refs], axis=0), jnp.concatenate([rows(r) for r in v_refs], axis=0), None)

    @pl.when(s == pl.num_programs(1) - 1)
    def _():
        for h in range(SB_HEADS):
            cols = slice(h * SB_HD, (h + 1) * SB_HD)
            o_ref[0, :, cols] = acc_ref[h * ntok:(h + 1) * ntok, cols].astype(o_ref.dtype)


def _sb_sample(q3, kn_pad, vn_pad, cache_k, cache_v, layer, page_table, bias_rows):
    nseq, ntok, _ = q3.shape
    npages = page_table.shape[1]
    npg = PAGES_PER_STEP
    nsteps = npages // npg
    t = kn_pad.shape[1]
    nrow = ntok * SB_HEADS

    def page_spec(j):
        return pl.BlockSpec((None, None, PAGE * SB_HEADS, SB_HD),
                            lambda b, s, pt: (layer, pt[b, (nsteps - 1 - s) * npg + j], 0, 0))

    seq_spec = lambda r: pl.BlockSpec((1, r, SB_WIDTH), lambda b, s, pt: (b, 0, 0))
    return pl.pallas_call(
        _sb_sample_kernel,
        grid_spec=pltpu.PrefetchScalarGridSpec(
            num_scalar_prefetch=1,
            grid=(nseq, nsteps),
            in_specs=[seq_spec(ntok), seq_spec(t), seq_spec(t)]
            + [page_spec(j) for j in range(npg)] + [page_spec(j) for j in range(npg)]
            + [pl.BlockSpec((SB_CHUNK, SB_CHUNK), lambda b, s, pt: (0, 0)),
               pl.BlockSpec((nrow, 1), lambda b, s, pt: (0, 0))],
            out_specs=seq_spec(ntok),
            scratch_shapes=[pltpu.VMEM((nrow, SB_WIDTH), BF16),
                            pltpu.VMEM((nrow, SB_WIDTH), F32),
                            pltpu.VMEM((nrow, 128), F32)],
        ),
        out_shape=jax.ShapeDtypeStruct(q3.shape, F32),
        compiler_params=_params("parallel", "arbitrary"),
        name="sb_sample",
    )(page_table, q3, kn_pad, vn_pad, *([cache_k] * npg), *([cache_v] * npg), _suffix_ones(SB_CHUNK), bias_rows)


def _mem_attn_kernel(q_ref, k_ref, v_ref, o_ref):
    for h in range(MEM_HEADS):
        cols = slice(h * MEM_HD, (h + 1) * MEM_HD)
        s = _dot_nt(q_ref[:, cols].astype(BF16), k_ref[:, cols].astype(BF16))
        e = jnp.exp(s - jnp.max(s, axis=-1, keepdims=True))
        p = e / jnp.sum(e, axis=-1, keepdims=True)
        o_ref[:, cols] = _dot(p.astype(BF16), v_ref[:, cols].astype(BF16)).astype(o_ref.dtype)


def _mem_attn(qm, mk3, mv3, tm):
    n = qm.shape[0]
    nb = mk3.shape[0]
    nt = n // (nb * tm)
    kv = pl.BlockSpec((None, N_MEM, MEM_WIDTH), lambda b, i: (b, 0, 0))
    return pl.pallas_call(
        _mem_attn_kernel,
        grid=(nb, nt),
        in_specs=[pl.BlockSpec((tm, MEM_WIDTH), lambda b, i: (b * nt + i, 0)), kv, kv],
        out_specs=pl.BlockSpec((tm, MEM_WIDTH), lambda b, i: (b * nt + i, 0)),
        out_shape=jax.ShapeDtypeStruct(qm.shape, qm.dtype),
        compiler_params=_params("parallel", "parallel"),
        name="mem_attn",
    )(qm, mk3, mv3)


def _merge_kernel(alpha, x_ref, g_ref, sb_ref, m_ref, wg_ref, bg_ref, wbg_ref, wbs_ref, wbm_ref, wo_ref,
                  lng_ref, lnb_ref, o_ref):
    x = x_ref[...]
    gates = jax.nn.sigmoid(_dot(x.astype(BF16), wg_ref[...]) + bg_ref[...])
    d = D_MODEL
    z = (gates[:, 0:d] * _dot(g_ref[...], wbg_ref[...])
         + gates[:, d:2 * d] * _dot(sb_ref[...], wbs_ref[...])
         + gates[:, 2 * d:3 * d] * _dot(m_ref[...], wbm_ref[...]))
    mix = _dot(z.astype(BF16), wo_ref[...])
    o_ref[...] = _layer_norm(alpha * x + mix, lng_ref[...], lnb_ref[...])


def _merge(alpha, x2, gout, sb, mo, w_gate, b_gate, w_bg, w_bs, w_bm, w_o, lng, lnb, tm=256):
    n = x2.shape[0]
    row = lambda a: pl.BlockSpec((tm, a.shape[1]), lambda i: (i, 0))
    full = lambda a: pl.BlockSpec(a.shape, lambda i: (0,) * a.ndim)
    ws = (w_gate, b_gate, w_bg, w_bs, w_bm, w_o, lng, lnb)
    return pl.pallas_call(
        functools.partial(_merge_kernel, alpha),
        grid=(n // tm,),
        in_specs=[row(x2), row(gout), row(sb), row(mo)] + [full(a) for a in ws],
        out_specs=row(x2),
        out_shape=jax.ShapeDtypeStruct(x2.shape, F32),
        compiler_params=_params("parallel"),
        name="merge",
    )(x2, gout, sb, mo, *ws)


def _cmpx(v, i, j):
    hi = jnp.maximum(v[i], v[j])
    lo = jnp.minimum(v[i], v[j])
    v[i], v[j] = hi, lo


def _sort16_desc(v):
    v = list(v)
    n = len(v)
    k = 2
    while k <= n:
        j = k // 2
        while j >= 1:
            for i in range(n):
                l = i ^ j
                if l > i:
                    if (i & k) == 0:
                        _cmpx(v, i, l)
                    else:
                        _cmpx(v, l, i)
            j //= 2
        k *= 2
    return v


def _merge_top16(a, b):
    n = len(a)
    c = []
    for i in range(n):
        j = n - 1 - i
        c.append(jnp.maximum(a[i], b[j]) if j < len(b) else a[i])
    j = n // 2
    while j >= 1:
        for i in range(n):
            if (i & j) == 0:
                _cmpx(c, i, i + j)
        j //= 2
    return c


def _top16_rows(s):
    g = _sort16_desc([s[8 * i:8 * i + 8, :] for i in range(N_KEYS // 8)])
    for shift in (4, 2, 1):
        g = _merge_top16(g, [pltpu.roll(x, shift, 0) for x in g])
    return g


def _tile_rows(a, reps):
    return jnp.concatenate([a] * reps, axis=0)


_PK_T1 = 0
_PK_T2 = PEER_TOPK
_PK_CNT = 2 * PEER_TOPK
_PK_INVZ = 3 * PEER_TOPK + 1
_PK_ROWS = 3 * PEER_TOPK + 2


def _peer_score_kernel(x_ref, wq_ref, sk_ref, xt_ref, nsel_ref, c_ref, rank_ref, p_ref, s1_s, s2_s, pk_ref):
    tm = x_ref.shape[0]
    xt = x_ref[...].T.astype(BF16)
    xt_ref[...] = xt
    k = PEER_TOPK
    sub = lax.broadcasted_iota(jnp.int32, (8, tm), 0)
    pk_ref[0:2 * k] = jnp.zeros((2 * k, 8, tm), F32)

    def scores(h, carry):
        qt = _dot(wq_ref[pl.ds(pl.multiple_of(h * 2 * PEER_DH, 2 * PEER_DH), 2 * PEER_DH), :], xt)
        s1 = _dot(sk_ref[2 * h], qt[:PEER_DH].astype(BF16))
        s2 = _dot(sk_ref[2 * h + 1], qt[PEER_DH:].astype(BF16))
        s1_s[h] = s1
        s2_s[h] = s2
        mine = sub == h
        for base, s in ((_PK_T1, s1), (_PK_T2, s2)):
            for a, t in enumerate(_top16_rows(s)):
                pk_ref[base + a] = jnp.where(mine, t, pk_ref[base + a])
        return carry

    lax.fori_loop(0, PEER_HEADS, scores, 0)

    t1 = [pk_ref[_PK_T1 + a] for a in range(k)]
    t2 = [pk_ref[_PK_T2 + a] for a in range(k)]
    top = [t1[0] + t2[b] for b in range(k)]
    for a in range(1, k):
        top = _merge_top16(top, [t1[a] + t2[b] for b in range(k // (a + 1))])
    tau = top[k - 1]
    m = t1[0] + t2[0]
    zsum = jnp.zeros_like(m)
    for a in range(k):
        cnt = jnp.zeros_like(m)
        for b in range(k // (a + 1)):
            c = t1[a] + t2[b]
            sel = c >= tau
            zsum = zsum + jnp.where(sel, jnp.exp(c - m), 0.0)
            cnt = cnt + jnp.where(sel, 1.0, 0.0)
        pk_ref[_PK_CNT + a] = cnt
    pk_ref[_PK_CNT + k] = jnp.zeros_like(m)
    pk_ref[_PK_INVZ] = 1.0 / zsum

    for h in range(PEER_HEADS):
        def row(i, cs):
            return jnp.broadcast_to(pk_ref[i, h:h + 1, cs], (N_KEYS, 128))

        for cc in range(tm // 128):
            cs = slice(cc * 128, (cc + 1) * 128)
            s2 = s2_s[h, :, cs]
            rank = jnp.zeros_like(s2)
            for a in range(k):
                rank = jnp.where(row(_PK_T2 + a, cs) > s2, float(a + 1), rank)
            rank_ref[h, :, cs] = rank.astype(BF16)
            p_ref[h, :, cs] = jnp.exp(s2 - row(_PK_T2, cs)).astype(BF16)
            s1 = s1_s[h, :, cs]
            nsel = row(_PK_CNT, cs)
            for a in range(k):
                nsel = jnp.where(row(_PK_T1 + a, cs) > s1, row(_PK_CNT + a + 1, cs), nsel)
            nsel_ref[h, :, cs] = nsel
            c_ref[h, :, cs] = jnp.exp(s1 - row(_PK_T1, cs)) * row(_PK_INVZ, cs)


def _peer_score(x1, wq_t, sk, tm):
    n = x1.shape[0]
    big = lambda dt: jax.ShapeDtypeStruct((PEER_HEADS, N_KEYS, n), dt)
    bspec = pl.BlockSpec((PEER_HEADS, N_KEYS, tm), lambda i: (0, 0, i))
    return pl.pallas_call(
        _peer_score_kernel,
        grid=(n // tm,),
        in_specs=[pl.BlockSpec((tm, D_MODEL), lambda i: (i, 0)),
                  pl.BlockSpec(wq_t.shape, lambda i: (0, 0)),
                  pl.BlockSpec(sk.shape, lambda i: (0, 0, 0))],
        out_specs=(pl.BlockSpec((D_MODEL, tm), lambda i: (0, i)), bspec, bspec, bspec, bspec),
        out_shape=(jax.ShapeDtypeStruct((D_MODEL, n), BF16), big(F32), big(F32), big(BF16), big(BF16)),
        scratch_shapes=[pltpu.VMEM((PEER_HEADS, N_KEYS, tm), F32),
                        pltpu.VMEM((PEER_HEADS, N_KEYS, tm), F32),
                        pltpu.VMEM((_PK_ROWS, 8, tm), F32)],
        compiler_params=_params("parallel"),
        name="peer_score",
    )(x1, wq_t, sk)


EXPERT_STEP = 2048
EXPERT_SUB = 256


def _peer_dense_kernel(alpha, x_ref, xt_ref, u_ref, vt_ref, nsel_ref, c_ref, rank_ref, p_ref,
                       lng_ref, lnb_ref, o_ref, acc_ref, act_ref, w_ref, rank_s, p_s):
    e = pl.program_id(1)
    tm = x_ref.shape[0]
    nhalf = EXPERT_SUB // N_KEYS
    nsub = EXPERT_STEP // EXPERT_SUB

    @pl.when(e == 0)
    def _():
        acc_ref[...] = jnp.zeros_like(acc_ref)
        rank_s[...] = rank_ref[...]
        p_s[...] = p_ref[...]

    def sel_row(ref, h, r, cs):
        tile = jnp.broadcast_to(ref[h, r:r + 1, cs], (16, 128)).astype(BF16)
        return jnp.concatenate([tile] * (N_KEYS // 16), axis=0)

    xt = xt_ref[...]
    for sub in range(nsub):
        rows = slice(sub * EXPERT_SUB, (sub + 1) * EXPERT_SUB)
        act_ref[rows, :] = jax.nn.gelu(_dot(u_ref[rows, :], xt).astype(BF16))
    for sub in range(nsub):
        for cc in range(tm // 128):
            cs = slice(cc * 128, (cc + 1) * 128)
            g = [jnp.zeros((N_KEYS, 128), BF16) for _ in range(nhalf)]
            for h in range(PEER_HEADS):
                rank = rank_s[h, :, cs]
                p = p_s[h, :, cs]
                for half in range(nhalf):
                    r = sub * nhalf + half
                    wsel = jnp.minimum(jnp.maximum(sel_row(nsel_ref, h, r, cs) - rank, 0), sel_row(c_ref, h, r, cs))
                    g[half] = g[half] + p * wsel
            for half in range(nhalf):
                er = slice(sub * EXPERT_SUB + half * N_KEYS, sub * EXPERT_SUB + (half + 1) * N_KEYS)
                w_ref[er, cs] = g[half] * act_ref[er, cs]
    acc_ref[...] += _dot(vt_ref[...], w_ref[...])

    @pl.when(e == pl.num_programs(1) - 1)
    def _():
        y = acc_ref[...].T
        o_ref[...] = _layer_norm(alpha * x_ref[...] + y, lng_ref[...], lnb_ref[...])


def _peer_dense(alpha, x1, xt, u_b, vt_b, nsel, c, rank, p, lng, lnb, tm):
    n = x1.shape[0]
    ne = u_b.shape[0] // EXPERT_STEP
    big = pl.BlockSpec((PEER_HEADS, N_KEYS, tm), lambda i, e: (0, 0, i))
    rowsel = pl.BlockSpec((PEER_HEADS, EXPERT_STEP // N_KEYS, tm), lambda i, e: (0, e, i))
    return pl.pallas_call(
        functools.partial(_peer_dense_kernel, alpha),
        grid=(n // tm, ne),
        in_specs=[pl.BlockSpec((tm, D_MODEL), lambda i, e: (i, 0)),
                  pl.BlockSpec((D_MODEL, tm), lambda i, e: (0, i)),
                  pl.BlockSpec((EXPERT_STEP, D_MODEL), lambda i, e: (e, 0)),
                  pl.BlockSpec((None, D_MODEL, EXPERT_STEP), lambda i, e: (e, 0, 0)),
                  rowsel, rowsel, big, big,
                  pl.BlockSpec(lng.shape, lambda i, e: (0, 0)),
                  pl.BlockSpec(lnb.shape, lambda i, e: (0, 0))],
        out_specs=pl.BlockSpec((tm, D_MODEL), lambda i, e: (i, 0)),
        out_shape=jax.ShapeDtypeStruct(x1.shape, F32),
        scratch_shapes=[pltpu.VMEM((D_MODEL, tm), F32),
                        pltpu.VMEM((EXPERT_STEP, tm), BF16),
                        pltpu.VMEM((EXPERT_STEP, tm), BF16),
                        pltpu.VMEM((PEER_HEADS, N_KEYS, tm), BF16),
                        pltpu.VMEM((PEER_HEADS, N_KEYS, tm), BF16)],
        compiler_params=_params("parallel", "arbitrary"),
        name="peer_dense",
    )(x1, xt, u_b, vt_b, nsel, c, rank, p, lng, lnb)


def _peer(alpha, x1, wq_t, sk, u_b, vt_b, lng, lnb, tm):
    xt, nsel, c, rank, p = _peer_score(x1, wq_t, sk, tm)
    return _peer_dense(alpha, x1, xt, u_b, vt_b, nsel, c, rank, p, lng, lnb, tm)


def kernel(x_prompt, x_sample, mem_prompt, cache_sb_k, cache_sb_v, page_table, cache_mem_k, cache_mem_v,
           w_in, b_gate, gmlp_ln_g, gmlp_ln_b, w_spatial, b_spatial, sb_bias, w_mem_kv, w_br_gmlp, w_br_sb,
           w_br_mem, w_out, ln1_g, ln1_b, peer_wq, peer_subkeys, peer_u, peer_v, ln2_g, ln2_b):
    depth = w_in.shape[0]
    alpha = float((2 * depth) ** 0.25)
    batch, seq, d = x_prompt.shape
    nseq, ntok, _ = x_sample.shape
    assert seq % SB_BLOCK == 0 and (nseq * ntok) % 256 == 0 and CHUNK % ntok == 0

    xp = x_prompt.reshape(batch * seq, d)
    xs = x_sample.reshape(nseq * ntok, d)
    outs = [[] for _ in range(7)]
    row2 = lambda a: a.reshape(1, -1)
    for l in range(depth):
        w_a = w_in[l][:, :A_WIDTH].astype(BF16)
        w_gate = w_in[l][:, A_WIDTH:].astype(BF16)
        bg = b_gate[l].reshape(1, N_BRANCH * D_MODEL)
        lng, lnb = row2(gmlp_ln_g[l]), row2(gmlp_ln_b[l])
        ws_p = w_spatial[l].astype(BF16)
        bs_p = jnp.repeat(b_spatial[l].T, CHUNK, axis=1)
        reps = CHUNK // ntok
        eye = jnp.eye(reps, dtype=F32)
        ws_s = jnp.einsum('ab,gij->gaibj', eye, w_spatial[l][:, :ntok, :ntok]).reshape(G_GROUPS, CHUNK, CHUNK).astype(BF16)
        bs_s = jnp.tile(bs_p[:ntok], (reps, 1))
        w_bg, w_bs, w_bm, w_o = (w.astype(BF16) for w in (w_br_gmlp[l], w_br_sb[l], w_br_mem[l], w_out[l]))
        l1g, l1b, l2g, l2b = row2(ln1_g[l]), row2(ln1_b[l]), row2(ln2_g[l]), row2(ln2_b[l])
        wq_t = peer_wq[l].T.astype(BF16)
        sk = peer_subkeys[l].reshape(PEER_HEADS * 2, N_KEYS, PEER_DH).astype(BF16)
        u_b = peer_u[l].astype(BF16)
        vt_b = peer_v[l].astype(BF16).reshape(-1, EXPERT_STEP, D_MODEL).transpose(0, 2, 1)
        bias = sb_bias[l].astype(F32)

        gout, _, qb, k, kb, vv, vvb, qm = _proj(xp, w_a, lng, lnb, ws_p, bs_p)
        mk, mv = _mem_kv(mem_prompt.reshape(-1, d), w_mem_kv[l].astype(BF16))
        mk3, mv3 = mk.reshape(batch, N_MEM, MEM_WIDTH), mv.reshape(batch, N_MEM, MEM_WIDTH)
        sb = _sb_prompt(qb, kb, vvb, bias, batch, seq)
        mo = _mem_attn(qm, mk3, mv3, tm=512)
        x1 = _merge(alpha, xp, gout, sb, mo, w_gate, bg, w_bg, w_bs, w_bm, w_o, l1g, l1b)
        xp = _peer(alpha, x1, wq_t, sk, u_b, vt_b, l2g, l2b, tm=512)
        outs[0].append(k.reshape(batch, seq, SB_HEADS, SB_HD))
        outs[1].append(vv.reshape(batch, seq, SB_HEADS, SB_HD))
        outs[2].append(mk.reshape(batch, N_MEM, MEM_HEADS, MEM_HD))
        outs[3].append(mv.reshape(batch, N_MEM, MEM_HEADS, MEM_HD))

        gout, v, qb, k, kb, vv, vvb, qm = _proj(xs, w_a, lng, lnb, ws_s, bs_s)
        pad = lambda a: jnp.pad(a.reshape(nseq, ntok, SB_WIDTH), ((0, 0), (0, SB_CHUNK - ntok), (0, 0)))
        bias_rows = jnp.repeat(bias, ntok).reshape(SB_HEADS * ntok, 1)
        pages = lambda c: c.reshape(c.shape[0], c.shape[1], PAGE * SB_HEADS, SB_HD)
        sb = _sb_sample(qb.astype(F32).reshape(nseq, ntok, SB_WIDTH), pad(kb), pad(vvb), pages(cache_sb_k),
                        pages(cache_sb_v), l, page_table, bias_rows).reshape(nseq * ntok, SB_WIDTH).astype(BF16)
        mo = _mem_attn(qm.astype(F32), cache_mem_k[l].reshape(nseq, N_MEM, MEM_WIDTH),
                       cache_mem_v[l].reshape(nseq, N_MEM, MEM_WIDTH), tm=ntok).astype(BF16)
        x1 = _merge(alpha, xs, gout, sb, mo, w_gate, bg, w_bg, w_bs, w_bm, w_o, l1g, l1b)
        xs = _peer(alpha, x1, wq_t, sk, u_b, vt_b, l2g, l2b, tm=256)
        outs[4].append(k.reshape(nseq, ntok, SB_HEADS, SB_HD))
        outs[5].append(vv.reshape(nseq, ntok, SB_HEADS, SB_HD))
        outs[6].append(v.reshape(nseq, ntok, G_GROUPS, CHUNK))

    return (xp.reshape(batch, seq, d), xs.reshape(nseq, ntok, d)) + tuple(jnp.stack(o) for o in outs)
```

```python
import functools

import jax
import jax.numpy as jnp
from jax import lax
from jax.experimental import pallas as pl
from jax.experimental.pallas import tpu as pltpu

F32 = jnp.float32
BF16 = jnp.bfloat16

D_MODEL = 1024
G_WIDTH = 512
G_GROUPS = 4
CHUNK = 128
SB_HEADS = 8
SB_HD = 128
SB_WIDTH = 1024
MEM_HEADS = 4
MEM_HD = 128
MEM_WIDTH = 512
N_MEM = 256
N_BRANCH = 3
A_WIDTH = 2 * G_WIDTH + 3 * SB_WIDTH + MEM_WIDTH
PEER_HEADS = 8
PEER_DH = 128
N_KEYS = 128
PEER_TOPK = 16
PAGE = 128
LN_EPS = 1e-5
SB_SCALE = SB_HD ** -0.5
MEM_SCALE = MEM_HD ** -0.5

SB_CHUNK = 256
SB_BLOCK = 512
SB_HP = 2
PAGES_PER_STEP = 8
VMEM_LIMIT = 52 * 1024 * 1024


def _dot(a, b):
    return jnp.dot(a, b, preferred_element_type=F32)


def _dot_nt(a, b):
    return lax.dot_general(a, b, (((1,), (1,)), ((), ())), preferred_element_type=F32)


def _layer_norm(x, g, b):
    mu = jnp.mean(x, axis=-1, keepdims=True)
    xc = x - mu
    var = jnp.mean(xc * xc, axis=-1, keepdims=True)
    return xc * lax.rsqrt(var + LN_EPS) * g + b


def _params(*sem):
    return pltpu.CompilerParams(dimension_semantics=sem, vmem_limit_bytes=VMEM_LIMIT)


def _proj_kernel(x_ref, w_ref, lng_ref, lnb_ref, ws_ref, bs_ref,
                 gout_ref, v_ref, qb_ref, k_ref, kb_ref, vv_ref, vvb_ref, qm_ref):
    tm = x_ref.shape[0]
    xb = x_ref[...].astype(BF16)
    u = jax.nn.gelu(_dot(xb, w_ref[:, 0:G_WIDTH]))
    gv = jax.nn.gelu(_dot(xb, w_ref[:, G_WIDTH:2 * G_WIDTH]))
    v = _layer_norm(gv, lng_ref[...], lnb_ref[...])
    v_ref[...] = v
    vb = v.astype(BF16)
    row = lax.broadcasted_iota(jnp.int32, (CHUNK, CHUNK), 0)
    col = lax.broadcasted_iota(jnp.int32, (CHUNK, CHUNK), 1)
    tril = col <= row
    for g in range(G_GROUPS):
        wg = jnp.where(tril, ws_ref[g], jnp.zeros((), BF16))
        cols = slice(g * CHUNK, (g + 1) * CHUNK)
        for c in range(tm // CHUNK):
            rows = slice(c * CHUNK, (c + 1) * CHUNK)
            s = _dot(wg, vb[rows, cols]) + bs_ref[:, cols]
            gout_ref[rows, cols] = (u[rows, cols] * s).astype(BF16)
    o = 2 * G_WIDTH
    qb_ref[...] = (_dot(xb, w_ref[:, o:o + SB_WIDTH]) * SB_SCALE).astype(BF16)
    k = _dot(xb, w_ref[:, o + SB_WIDTH:o + 2 * SB_WIDTH])
    k_ref[...] = k
    kb_ref[...] = k.astype(BF16)
    vv = _dot(xb, w_ref[:, o + 2 * SB_WIDTH:o + 3 * SB_WIDTH])
    vv_ref[...] = vv
    vvb_ref[...] = vv.astype(BF16)
    o = o + 3 * SB_WIDTH
    qm_ref[...] = (_dot(xb, w_ref[:, o:o + MEM_WIDTH]) * MEM_SCALE).astype(BF16)


def _proj(x2, w_a, lng, lnb, ws, bs_full, tm=256):
    n = x2.shape[0]
    row = lambda w: pl.BlockSpec((tm, w), lambda i: (i, 0))
    full = lambda a: pl.BlockSpec(a.shape, lambda i: (0,) * a.ndim)
    out_shape = (
        jax.ShapeDtypeStruct((n, G_WIDTH), BF16),
        jax.ShapeDtypeStruct((n, G_WIDTH), F32),
        jax.ShapeDtypeStruct((n, SB_WIDTH), BF16),
        jax.ShapeDtypeStruct((n, SB_WIDTH), F32),
        jax.ShapeDtypeStruct((n, SB_WIDTH), BF16),
        jax.ShapeDtypeStruct((n, SB_WIDTH), F32),
        jax.ShapeDtypeStruct((n, SB_WIDTH), BF16),
        jax.ShapeDtypeStruct((n, MEM_WIDTH), BF16),
    )
    return pl.pallas_call(
        _proj_kernel,
        grid=(n // tm,),
        in_specs=[row(D_MODEL), full(w_a), full(lng), full(lnb), full(ws), full(bs_full)],
        out_specs=tuple(row(s.shape[1]) for s in out_shape),
        out_shape=out_shape,
        compiler_params=_params("parallel"),
        name="proj",
    )(x2, w_a, lng, lnb, ws, bs_full)


def _matmul2_kernel(x_ref, w_ref, a_ref, b_ref):
    y = _dot(x_ref[...].astype(BF16), w_ref[...])
    h = a_ref.shape[1]
    a_ref[...] = y[:, :h]
    b_ref[...] = y[:, h:]


def _mem_kv(mem2, w, tm=256):
    n = mem2.shape[0]
    h = w.shape[1] // 2
    return pl.pallas_call(
        _matmul2_kernel,
        grid=(n // tm,),
        in_specs=[pl.BlockSpec((tm, D_MODEL), lambda i: (i, 0)), pl.BlockSpec(w.shape, lambda i: (0, 0))],
        out_specs=(pl.BlockSpec((tm, h), lambda i: (i, 0)), pl.BlockSpec((tm, h), lambda i: (i, 0))),
        out_shape=(jax.ShapeDtypeStruct((n, h), F32), jax.ShapeDtypeStruct((n, h), F32)),
        compiler_params=_params("parallel"),
        name="mem_kv",
    )(mem2, w)


def _sb_block(q, kb, vb, r_mat, bias, carry, causal):
    c = SB_CHUNK
    z = _dot_nt(q, kb) + bias
    sp = jnp.maximum(z, 0.0) + jnp.log(1.0 + jnp.exp(-jnp.abs(z)))
    if causal is not None:
        sp = jnp.where(causal, sp, 0.0)
    spb = sp.astype(BF16)
    nchunk = kb.shape[0] // c
    local = [_dot(spb[:, j * c:(j + 1) * c], r_mat) for j in range(nchunk)]
    pieces = [None] * nchunk
    for j in range(nchunk - 1, -1, -1):
        pieces[j] = jnp.exp(z[:, j * c:(j + 1) * c] - local[j] - carry)
        carry = carry + local[j][:, 0:1]
    a = pieces[0] if nchunk == 1 else jnp.concatenate(pieces, axis=1)
    if causal is not None:
        a = jnp.where(causal, a, 0.0)
    return _dot(a.astype(BF16), vb), carry


def _sb_prompt_kernel(bias_ref, q_ref, k_ref, v_ref, r_ref, o_ref):
    hg = pl.program_id(1)
    i = pl.program_id(2)
    t = SB_BLOCK
    r_mat = r_ref[...]
    row = lax.broadcasted_iota(jnp.int32, (t, t), 0)
    col = lax.broadcasted_iota(jnp.int32, (t, t), 1)
    heads = [(slice(j * SB_HD, (j + 1) * SB_HD), bias_ref[hg * SB_HP + j]) for j in range(SB_HP)]

    def visit(start, state, causal):
        out = []
        for (cols, bias), (acc, carry) in zip(heads, state):
            d, carry = _sb_block(q_ref[:, cols], k_ref[pl.ds(start, t), cols], v_ref[pl.ds(start, t), cols],
                                 r_mat, bias, carry, causal)
            out.append((acc + d, carry))
        return tuple(out)

    zero = (jnp.zeros((t, SB_HD), F32), jnp.zeros((t, 1), F32))
    state = visit(pl.multiple_of(i * t, t), (zero,) * SB_HP, col < row)
    def two_blocks(j, st):
        st = visit(pl.multiple_of((i - 1 - 2 * j) * t, t), st, None)
        return visit(pl.multiple_of((i - 2 - 2 * j) * t, t), st, None)

    state = lax.fori_loop(0, i // 2, two_blocks, state)
    state = lax.cond(i % 2 == 1, lambda st: visit(0, st, None), lambda st: st, state)
    for (cols, _), (acc, _) in zip(heads, state):
        o_ref[:, cols] = acc.astype(o_ref.dtype)


def _suffix_ones(n):
    j = lax.broadcasted_iota(jnp.int32, (n, n), 0)
    s = lax.broadcasted_iota(jnp.int32, (n, n), 1)
    return (j >= s).astype(BF16)


def _sb_prompt(qb, kb, vb, bias, batch, seq):
    t = SB_BLOCK
    nq = seq // t
    w = SB_HP * SB_HD
    return pl.pallas_call(
        _sb_prompt_kernel,
        grid_spec=pltpu.PrefetchScalarGridSpec(
            num_scalar_prefetch=1,
            grid=(batch, SB_HEADS // SB_HP, nq),
            in_specs=[
                pl.BlockSpec((t, w), lambda b, h, i, bias: (b * nq + i, h)),
                pl.BlockSpec((seq, w), lambda b, h, i, bias: (b, h)),
                pl.BlockSpec((seq, w), lambda b, h, i, bias: (b, h)),
                pl.BlockSpec((SB_CHUNK, SB_CHUNK), lambda b, h, i, bias: (0, 0)),
            ],
            out_specs=pl.BlockSpec((t, w), lambda b, h, i, bias: (b * nq + i, h)),
        ),
        out_shape=jax.ShapeDtypeStruct(qb.shape, BF16),
        compiler_params=_params("parallel", "parallel", "arbitrary"),
        name="sb_prompt",
    )(bias, qb, kb, vb, _suffix_ones(SB_CHUNK))


def _sb_sample_kernel(pt_ref, q_ref, kn_ref, vn_ref, *refs):
    npg = PAGES_PER_STEP
    k_refs = refs[:npg]
    v_refs = refs[npg:2 * npg]
    r_ref, bias_ref, o_ref, qbd_ref, acc_ref, carry_ref = refs[2 * npg:]
    s = pl.program_id(1)
    t = kn_ref.shape[1]
    nrow = q_ref.shape[1] * SB_HEADS
    ntok = q_ref.shape[1]
    r_mat = r_ref[...]
    bias = bias_ref[...]

    def visit(kb, vb, causal):
        d, c = _sb_block(qbd_ref[...], kb, vb, r_mat, bias, carry_ref[:, 0:1], causal)
        acc_ref[...] += d
        carry_ref[...] = jnp.broadcast_to(c, carry_ref.shape)

    @pl.when(s == 0)
    def _():
        qrep = jnp.concatenate([q_ref[0]] * SB_HEADS, axis=0)
        rh = lax.broadcasted_iota(jnp.int32, qrep.shape, 0) // ntok
        ch = lax.broadcasted_iota(jnp.int32, qrep.shape, 1) // SB_HD
        qbd_ref[...] = jnp.where(rh == ch, qrep, 0.0).astype(BF16)
        acc_ref[...] = jnp.zeros_like(acc_ref)
        carry_ref[...] = jnp.zeros_like(carry_ref)
        tq = lax.broadcasted_iota(jnp.int32, (nrow, t), 0) % ntok
        tk = lax.broadcasted_iota(jnp.int32, (nrow, t), 1)
        visit(kn_ref[0], vn_ref[0], tk < tq)

    def rows(page_ref):
        return jnp.concatenate([page_ref[pl.ds(h, PAGE, stride=SB_HEADS), :] for h in range(SB_HEADS)],
                               axis=1).astype(BF16)

    visit(jnp.concatenate([rows(r) for r in k_refs], axis=0), jnp.concatenate([rows(r) for r in v_refs], axis=0), None)

    @pl.when(s == pl.num_programs(1) - 1)
    def _():
        for h in range(SB_HEADS):
            cols = slice(h * SB_HD, (h + 1) * SB_HD)
            o_ref[0, :, cols] = acc_ref[h * ntok:(h + 1) * ntok, cols].astype(o_ref.dtype)


def _sb_sample(q3, kn_pad, vn_pad, cache_k, cache_v, layer, page_table, bias_rows):
    nseq, ntok, _ = q3.shape
    npages = page_table.shape[1]
    npg = PAGES_PER_STEP
    nsteps = npages // npg
    t = kn_pad.shape[1]
    nrow = ntok * SB_HEADS

    def page_spec(j):
        return pl.BlockSpec((None, None, PAGE * SB_HEADS, SB_HD),
                            lambda b, s, pt: (layer, pt[b, (nsteps - 1 - s) * npg + j], 0, 0))

    seq_spec = lambda r: pl.BlockSpec((1, r, SB_WIDTH), lambda b, s, pt: (b, 0, 0))
    return pl.pallas_call(
        _sb_sample_kernel,
        grid_spec=pltpu.PrefetchScalarGridSpec(
            num_scalar_prefetch=1,
            grid=(nseq, nsteps),
            in_specs=[seq_spec(ntok), seq_spec(t), seq_spec(t)]
            + [page_spec(j) for j in range(npg)] + [page_spec(j) for j in range(npg)]
            + [pl.BlockSpec((SB_CHUNK, SB_CHUNK), lambda b, s, pt: (0, 0)),
               pl.BlockSpec((nrow, 1), lambda b, s, pt: (0, 0))],
            out_specs=seq_spec(ntok),
            scratch_shapes=[pltpu.VMEM((nrow, SB_WIDTH), BF16),
                            pltpu.VMEM((nrow, SB_WIDTH), F32),
                            pltpu.VMEM((nrow, 128), F32)],
        ),
        out_shape=jax.ShapeDtypeStruct(q3.shape, F32),
        compiler_params=_params("parallel", "arbitrary"),
        name="sb_sample",
    )(page_table, q3, kn_pad, vn_pad, *([cache_k] * npg), *([cache_v] * npg), _suffix_ones(SB_CHUNK), bias_rows)


def _mem_attn_kernel(q_ref, k_ref, v_ref, o_ref):
    group = k_ref.shape[0]
    tm = q_ref.shape[0] // group
    for g in range(group):
        rows = slice(g * tm, (g + 1) * tm)
        for h in range(MEM_HEADS):
            cols = slice(h * MEM_HD, (h + 1) * MEM_HD)
            s = _dot_nt(q_ref[rows, cols].astype(BF16), k_ref[g, :, cols].astype(BF16))
            e = jnp.exp(s - jnp.max(s, axis=-1, keepdims=True))
            p = e / jnp.sum(e, axis=-1, keepdims=True)
            o_ref[rows, cols] = _dot(p.astype(BF16), v_ref[g, :, cols].astype(BF16)).astype(o_ref.dtype)


def _mem_attn(qm, mk3, mv3, tm, group=1):
    n = qm.shape[0]
    nb = mk3.shape[0] // group
    nt = n // (nb * group * tm)
    assert mk3.shape[0] % group == 0 and (group == 1 or nt == 1)
    kv = pl.BlockSpec((group, N_MEM, MEM_WIDTH), lambda b, i: (b, 0, 0))
    qo = pl.BlockSpec((group * tm, MEM_WIDTH), lambda b, i: (b * nt + i, 0))
    return pl.pallas_call(
        _mem_attn_kernel,
        grid=(nb, nt),
        in_specs=[qo, kv, kv],
        out_specs=qo,
        out_shape=jax.ShapeDtypeStruct(qm.shape, qm.dtype),
        compiler_params=_params("parallel", "parallel"),
        name="mem_attn",
    )(qm, mk3, mv3)


def _merge_kernel(alpha, x_ref, g_ref, sb_ref, m_ref, wg_ref, bg_ref, wbg_ref, wbs_ref, wbm_ref, wo_ref,
                  lng_ref, lnb_ref, o_ref):
    x = x_ref[...]
    gates = jax.nn.sigmoid(_dot(x.astype(BF16), wg_ref[...]) + bg_ref[...])
    d = D_MODEL
    z = (gates[:, 0:d] * _dot(g_ref[...], wbg_ref[...])
         + gates[:, d:2 * d] * _dot(sb_ref[...], wbs_ref[...])
         + gates[:, 2 * d:3 * d] * _dot(m_ref[...], wbm_ref[...]))
    mix = _dot(z.astype(BF16), wo_ref[...])
    o_ref[...] = _layer_norm(alpha * x + mix, lng_ref[...], lnb_ref[...])


def _merge(alpha, x2, gout, sb, mo, w_gate, b_gate, w_bg, w_bs, w_bm, w_o, lng, lnb, tm=256):
    n = x2.shape[0]
    row = lambda a: pl.BlockSpec((tm, a.shape[1]), lambda i: (i, 0))
    full = lambda a: pl.BlockSpec(a.shape, lambda i: (0,) * a.ndim)
    ws = (w_gate, b_gate, w_bg, w_bs, w_bm, w_o, lng, lnb)
    return pl.pallas_call(
        functools.partial(_merge_kernel, alpha),
        grid=(n // tm,),
        in_specs=[row(x2), row(gout), row(sb), row(mo)] + [full(a) for a in ws],
        out_specs=row(x2),
        out_shape=jax.ShapeDtypeStruct(x2.shape, F32),
        compiler_params=_params("parallel"),
        name="merge",
    )(x2, gout, sb, mo, *ws)


def _cmpx(v, i, j):
    hi = jnp.maximum(v[i], v[j])
    lo = jnp.minimum(v[i], v[j])
    v[i], v[j] = hi, lo


def _sort16_desc(v):
    v = list(v)
    n = len(v)
    k = 2
    while k <= n:
        j = k // 2
        while j >= 1:
            for i in range(n):
                l = i ^ j
                if l > i:
                    if (i & k) == 0:
                        _cmpx(v, i, l)
                    else:
                        _cmpx(v, l, i)
            j //= 2
        k *= 2
    return v


def _merge_top16(a, b):
    n = len(a)
    c = []
    for i in range(n):
        j = n - 1 - i
        c.append(jnp.maximum(a[i], b[j]) if j < len(b) else a[i])
    j = n // 2
    while j >= 1:
        for i in range(n):
            if (i & j) == 0:
                _cmpx(c, i, i + j)
        j //= 2
    return c


def _top16_rows(s):
    g = _sort16_desc([s[8 * i:8 * i + 8, :] for i in range(N_KEYS // 8)])
    for shift in (4, 2, 1):
        g = _merge_top16(g, [pltpu.roll(x, shift, 0) for x in g])
    return g


def _tile_rows(a, reps):
    return jnp.concatenate([a] * reps, axis=0)


_PK_T1 = 0
_PK_T2 = PEER_TOPK
_PK_CNT = 2 * PEER_TOPK
_PK_INVZ = 3 * PEER_TOPK + 1
_PK_ROWS = 3 * PEER_TOPK + 2


def _peer_score_kernel(x_ref, wq_ref, sk_ref, xt_ref, nsel_ref, c_ref, rank_ref, p_ref, s1_s, s2_s, pk_ref):
    tm = x_ref.shape[0]
    xt = x_ref[...].T.astype(BF16)
    xt_ref[...] = xt
    k = PEER_TOPK
    sub = lax.broadcasted_iota(jnp.int32, (8, tm), 0)
    pk_ref[0:2 * k] = jnp.zeros((2 * k, 8, tm), F32)

    def scores(h, carry):
        qt = _dot(wq_ref[pl.ds(pl.multiple_of(h * 2 * PEER_DH, 2 * PEER_DH), 2 * PEER_DH), :], xt)
        s1 = _dot(sk_ref[2 * h], qt[:PEER_DH].astype(BF16))
        s2 = _dot(sk_ref[2 * h + 1], qt[PEER_DH:].astype(BF16))
        s1_s[h] = s1
        s2_s[h] = s2
        mine = sub == h
        for base, s in ((_PK_T1, s1), (_PK_T2, s2)):
            for a, t in enumerate(_top16_rows(s)):
                pk_ref[base + a] = jnp.where(mine, t, pk_ref[base + a])
        return carry

    lax.fori_loop(0, PEER_HEADS, scores, 0, unroll=2)

    t1 = [pk_ref[_PK_T1 + a] for a in range(k)]
    t2 = [pk_ref[_PK_T2 + a] for a in range(k)]
    top = [t1[0] + t2[b] for b in range(k)]
    for a in range(1, k):
        top = _merge_top16(top, [t1[a] + t2[b] for b in range(k // (a + 1))])
    tau = top[k - 1]
    m = t1[0] + t2[0]
    zsum = jnp.zeros_like(m)
    for a in range(k):
        cnt = jnp.zeros_like(m)
        for b in range(k // (a + 1)):
            c = t1[a] + t2[b]
            sel = c >= tau
            zsum = zsum + jnp.where(sel, jnp.exp(c - m), 0.0)
            cnt = cnt + jnp.where(sel, 1.0, 0.0)
        pk_ref[_PK_CNT + a] = cnt
    pk_ref[_PK_CNT + k] = jnp.zeros_like(m)
    pk_ref[_PK_INVZ] = 1.0 / zsum

    for h in range(PEER_HEADS):
        def row(i, cs):
            return jnp.broadcast_to(pk_ref[i, h:h + 1, cs], (N_KEYS, 128))

        for cc in range(tm // 128):
            cs = slice(cc * 128, (cc + 1) * 128)
            s2 = s2_s[h, :, cs]
            rank = jnp.zeros_like(s2)
            for a in range(k):
                rank = jnp.where(row(_PK_T2 + a, cs) > s2, float(a + 1), rank)
            rank_ref[h, :, cs] = rank.astype(BF16)
            p_ref[h, :, cs] = jnp.exp(s2 - row(_PK_T2, cs)).astype(BF16)
            s1 = s1_s[h, :, cs]
            nsel = row(_PK_CNT, cs)
            for a in range(k):
                nsel = jnp.where(row(_PK_T1 + a, cs) > s1, row(_PK_CNT + a + 1, cs), nsel)
            nsel_ref[h, :, cs] = nsel
            c_ref[h, :, cs] = jnp.exp(s1 - row(_PK_T1, cs)) * row(_PK_INVZ, cs)


def _peer_score(x1, wq_t, sk, tm):
    n = x1.shape[0]
    big = lambda dt: jax.ShapeDtypeStruct((PEER_HEADS, N_KEYS, n), dt)
    bspec = pl.BlockSpec((PEER_HEADS, N_KEYS, tm), lambda i: (0, 0, i))
    return pl.pallas_call(
        _peer_score_kernel,
        grid=(n // tm,),
        in_specs=[pl.BlockSpec((tm, D_MODEL), lambda i: (i, 0)),
                  pl.BlockSpec(wq_t.shape, lambda i: (0, 0)),
                  pl.BlockSpec(sk.shape, lambda i: (0, 0, 0))],
        out_specs=(pl.BlockSpec((D_MODEL, tm), lambda i: (0, i)), bspec, bspec, bspec, bspec),
        out_shape=(jax.ShapeDtypeStruct((D_MODEL, n), BF16), big(F32), big(F32), big(BF16), big(BF16)),
        scratch_shapes=[pltpu.VMEM((PEER_HEADS, N_KEYS, tm), F32),
                        pltpu.VMEM((PEER_HEADS, N_KEYS, tm), F32),
                        pltpu.VMEM((_PK_ROWS, 8, tm), F32)],
        compiler_params=_params("parallel"),
        name="peer_score",
    )(x1, wq_t, sk)


EXPERT_STEP = 2048
EXPERT_SUB = 256


def _peer_dense_kernel(alpha, x_ref, xt_ref, u_ref, vt_ref, nsel_ref, c_ref, rank_ref, p_ref,
                       lng_ref, lnb_ref, o_ref, acc_ref, act_ref, w_ref, rank_s, p_s):
    e = pl.program_id(1)
    tm = x_ref.shape[0]
    nhalf = EXPERT_SUB // N_KEYS
    nsub = EXPERT_STEP // EXPERT_SUB

    @pl.when(e == 0)
    def _():
        acc_ref[...] = jnp.zeros_like(acc_ref)
        rank_s[...] = rank_ref[...]
        p_s[...] = p_ref[...]

    def sel_row(ref, h, r, cs):
        tile = jnp.broadcast_to(ref[h, r:r + 1, cs], (16, 128)).astype(BF16)
        return jnp.concatenate([tile] * (N_KEYS // 16), axis=0)

    xt = xt_ref[...]
    for sub in range(nsub):
        rows = slice(sub * EXPERT_SUB, (sub + 1) * EXPERT_SUB)
        act_ref[rows, :] = jax.nn.gelu(_dot(u_ref[rows, :], xt).astype(BF16))
    for sub in range(nsub):
        for cc in range(tm // 128):
            cs = slice(cc * 128, (cc + 1) * 128)
            g = [jnp.zeros((N_KEYS, 128), BF16) for _ in range(nhalf)]
            for h in range(PEER_HEADS):
                rank = rank_s[h, :, cs]
                p = p_s[h, :, cs]
                for half in range(nhalf):
                    r = sub * nhalf + half
                    wsel = jnp.minimum(jnp.maximum(sel_row(nsel_ref, h, r, cs) - rank, 0), sel_row(c_ref, h, r, cs))
                    g[half] = g[half] + p * wsel
            for half in range(nhalf):
                er = slice(sub * EXPERT_SUB + half * N_KEYS, sub * EXPERT_SUB + (half + 1) * N_KEYS)
                w_ref[er, cs] = g[half] * act_ref[er, cs]
    acc_ref[...] += _dot(vt_ref[...], w_ref[...])

    @pl.when(e == pl.num_programs(1) - 1)
    def _():
        y = acc_ref[...].T
        o_ref[...] = _layer_norm(alpha * x_ref[...] + y, lng_ref[...], lnb_ref[...])


def _peer_dense(alpha, x1, xt, u_b, vt_b, nsel, c, rank, p, lng, lnb, tm):
    n = x1.shape[0]
    ne = u_b.shape[0] // EXPERT_STEP
    big = pl.BlockSpec((PEER_HEADS, N_KEYS, tm), lambda i, e: (0, 0, i))
    rowsel = pl.BlockSpec((PEER_HEADS, EXPERT_STEP // N_KEYS, tm), lambda i, e: (0, e, i))
    return pl.pallas_call(
        functools.partial(_peer_dense_kernel, alpha),
        grid=(n // tm, ne),
        in_specs=[pl.BlockSpec((tm, D_MODEL), lambda i, e: (i, 0)),
                  pl.BlockSpec((D_MODEL, tm), lambda i, e: (0, i)),
                  pl.BlockSpec((EXPERT_STEP, D_MODEL), lambda i, e: (e, 0)),
                  pl.BlockSpec((None, D_MODEL, EXPERT_STEP), lambda i, e: (e, 0, 0)),
                  rowsel, rowsel, big, big,
                  pl.BlockSpec(lng.shape, lambda i, e: (0, 0)),
                  pl.BlockSpec(lnb.shape, lambda i, e: (0, 0))],
        out_specs=pl.BlockSpec((tm, D_MODEL), lambda i, e: (i, 0)),
        out_shape=jax.ShapeDtypeStruct(x1.shape, F32),
        scratch_shapes=[pltpu.VMEM((D_MODEL, tm), F32),
                        pltpu.VMEM((EXPERT_STEP, tm), BF16),
                        pltpu.VMEM((EXPERT_STEP, tm), BF16),
                        pltpu.VMEM((PEER_HEADS, N_KEYS, tm), BF16),
                        pltpu.VMEM((PEER_HEADS, N_KEYS, tm), BF16)],
        compiler_params=_params("parallel", "arbitrary"),
        name="peer_dense",
    )(x1, xt, u_b, vt_b, nsel, c, rank, p, lng, lnb)


def _peer(alpha, x1, wq_t, sk, u_b, vt_b, lng, lnb, tm):
    xt, nsel, c, rank, p = _peer_score(x1, wq_t, sk, tm)
    return _peer_dense(alpha, x1, xt, u_b, vt_b, nsel, c, rank, p, lng, lnb, tm)


def kernel(x_prompt, x_sample, mem_prompt, cache_sb_k, cache_sb_v, page_table, cache_mem_k, cache_mem_v,
           w_in, b_gate, gmlp_ln_g, gmlp_ln_b, w_spatial, b_spatial, sb_bias, w_mem_kv, w_br_gmlp, w_br_sb,
           w_br_mem, w_out, ln1_g, ln1_b, peer_wq, peer_subkeys, peer_u, peer_v, ln2_g, ln2_b):
    depth = w_in.shape[0]
    alpha = float((2 * depth) ** 0.25)
    batch, seq, d = x_prompt.shape
    nseq, ntok, _ = x_sample.shape
    assert seq % SB_BLOCK == 0 and (nseq * ntok) % 256 == 0 and CHUNK % ntok == 0

    xp = x_prompt.reshape(batch * seq, d)
    xs = x_sample.reshape(nseq * ntok, d)
    outs = [[] for _ in range(7)]
    row2 = lambda a: a.reshape(1, -1)
    for l in range(depth):
        w_a = w_in[l][:, :A_WIDTH].astype(BF16)
        w_gate = w_in[l][:, A_WIDTH:].astype(BF16)
        bg = b_gate[l].reshape(1, N_BRANCH * D_MODEL)
        lng, lnb = row2(gmlp_ln_g[l]), row2(gmlp_ln_b[l])
        ws_p = w_spatial[l].astype(BF16)
        bs_p = jnp.repeat(b_spatial[l].T, CHUNK, axis=1)
        reps = CHUNK // ntok
        eye = jnp.eye(reps, dtype=F32)
        ws_s = jnp.einsum('ab,gij->gaibj', eye, w_spatial[l][:, :ntok, :ntok]).reshape(G_GROUPS, CHUNK, CHUNK).astype(BF16)
        bs_s = jnp.tile(bs_p[:ntok], (reps, 1))
        w_bg, w_bs, w_bm, w_o = (w.astype(BF16) for w in (w_br_gmlp[l], w_br_sb[l], w_br_mem[l], w_out[l]))
        l1g, l1b, l2g, l2b = row2(ln1_g[l]), row2(ln1_b[l]), row2(ln2_g[l]), row2(ln2_b[l])
        wq_t = peer_wq[l].T.astype(BF16)
        sk = peer_subkeys[l].reshape(PEER_HEADS * 2, N_KEYS, PEER_DH).astype(BF16)
        u_b = peer_u[l].astype(BF16)
        vt_b = peer_v[l].astype(BF16).reshape(-1, EXPERT_STEP, D_MODEL).transpose(0, 2, 1)
        bias = sb_bias[l].astype(F32)

        gout, _, qb, k, kb, vv, vvb, qm = _proj(xp, w_a, lng, lnb, ws_p, bs_p)
        mk, mv = _mem_kv(mem_prompt.reshape(-1, d), w_mem_kv[l].astype(BF16))
        mk3, mv3 = mk.reshape(batch, N_MEM, MEM_WIDTH), mv.reshape(batch, N_MEM, MEM_WIDTH)
        sb = _sb_prompt(qb, kb, vvb, bias, batch, seq)
        mo = _mem_attn(qm, mk3, mv3, tm=512)
        x1 = _merge(alpha, xp, gout, sb, mo, w_gate, bg, w_bg, w_bs, w_bm, w_o, l1g, l1b)
        xp = _peer(alpha, x1, wq_t, sk, u_b, vt_b, l2g, l2b, tm=512)
        outs[0].append(k.reshape(batch, seq, SB_HEADS, SB_HD))
        outs[1].append(vv.reshape(batch, seq, SB_HEADS, SB_HD))
        outs[2].append(mk.reshape(batch, N_MEM, MEM_HEADS, MEM_HD))
        outs[3].append(mv.reshape(batch, N_MEM, MEM_HEADS, MEM_HD))

        gout, v, qb, k, kb, vv, vvb, qm = _proj(xs, w_a, lng, lnb, ws_s, bs_s)
        pad = lambda a: jnp.pad(a.reshape(nseq, ntok, SB_WIDTH), ((0, 0), (0, SB_CHUNK - ntok), (0, 0)))
        bias_rows = jnp.repeat(bias, ntok).reshape(SB_HEADS * ntok, 1)
        pages = lambda c: c.reshape(c.shape[0], c.shape[1], PAGE * SB_HEADS, SB_HD)
        sb = _sb_sample(qb.astype(F32).reshape(nseq, ntok, SB_WIDTH), pad(kb), pad(vvb), pages(cache_sb_k),
                        pages(cache_sb_v), l, page_table, bias_rows).reshape(nseq * ntok, SB_WIDTH).astype(BF16)
        mo = _mem_attn(qm.astype(F32), cache_mem_k[l].reshape(nseq, N_MEM, MEM_WIDTH),
                       cache_mem_v[l].reshape(nseq, N_MEM, MEM_WIDTH), tm=ntok, group=8).astype(BF16)
        x1 = _merge(alpha, xs, gout, sb, mo, w_gate, bg, w_bg, w_bs, w_bm, w_o, l1g, l1b)
        xs = _peer(alpha, x1, wq_t, sk, u_b, vt_b, l2g, l2b, tm=256)
        outs[4].append(k.reshape(nseq, ntok, SB_HEADS, SB_HD))
        outs[5].append(vv.reshape(nseq, ntok, SB_HEADS, SB_HD))
        outs[6].append(v.reshape(nseq, ntok, G_GROUPS, CHUNK))

    return (xp.reshape(batch, seq, d), xs.reshape(nseq, ntok, d)) + tuple(jnp.stack(o) for o in outs)
```

```python
import functools

import jax
import jax.numpy as jnp
from jax import lax
from jax.experimental import pallas as pl
from jax.experimental.pallas import tpu as pltpu

F32 = jnp.float32
BF16 = jnp.bfloat16

D_MODEL = 1024
G_WIDTH = 512
G_GROUPS = 4
CHUNK = 128
SB_HEADS = 8
SB_HD = 128
SB_WIDTH = 1024
MEM_HEADS = 4
MEM_HD = 128
MEM_WIDTH = 512
N_MEM = 256
N_BRANCH = 3
A_WIDTH = 2 * G_WIDTH + 3 * SB_WIDTH + MEM_WIDTH
PEER_HEADS = 8
PEER_DH = 128
N_KEYS = 128
PEER_TOPK = 16
PAGE = 128
LN_EPS = 1e-5
SB_SCALE = SB_HD ** -0.5
MEM_SCALE = MEM_HD ** -0.5

SB_CHUNK = 256
SB_BLOCK = 512
SB_HP = 2
PAGES_PER_STEP = 8
VMEM_LIMIT = 52 * 1024 * 1024


def _dot(a, b):
    return jnp.dot(a, b, preferred_element_type=F32)


def _dot_nt(a, b):
    return lax.dot_general(a, b, (((1,), (1,)), ((), ())), preferred_element_type=F32)


def _layer_norm(x, g, b):
    mu = jnp.mean(x, axis=-1, keepdims=True)
    xc = x - mu
    var = jnp.mean(xc * xc, axis=-1, keepdims=True)
    return xc * lax.rsqrt(var + LN_EPS) * g + b


def _params(*sem):
    return pltpu.CompilerParams(dimension_semantics=sem, vmem_limit_bytes=VMEM_LIMIT)


def _store_head_rows(ref, y, heads):
    rows, hd = y.shape[0], y.shape[1] // heads
    for h in range(heads):
        ref[pl.ds(h, rows, stride=heads), :] = y[:, h * hd:(h + 1) * hd]


def _load_head_rows(ref, h, heads):
    return ref[pl.ds(h, ref.shape[0] // heads, stride=heads), :]


def _proj_kernel(x_ref, w_ref, lng_ref, lnb_ref, ws_ref, bs_ref,
                 gout_ref, v_ref, qb_ref, k_ref, kb_ref, vv_ref, vvb_ref, qm_ref):
    tm = x_ref.shape[0]
    xb = x_ref[...].astype(BF16)
    u = jax.nn.gelu(_dot(xb, w_ref[:, 0:G_WIDTH]))
    gv = jax.nn.gelu(_dot(xb, w_ref[:, G_WIDTH:2 * G_WIDTH]))
    v = _layer_norm(gv, lng_ref[...], lnb_ref[...])
    v_ref[...] = v
    vb = v.astype(BF16)
    row = lax.broadcasted_iota(jnp.int32, (CHUNK, CHUNK), 0)
    col = lax.broadcasted_iota(jnp.int32, (CHUNK, CHUNK), 1)
    tril = col <= row
    for g in range(G_GROUPS):
        wg = jnp.where(tril, ws_ref[g], jnp.zeros((), BF16))
        cols = slice(g * CHUNK, (g + 1) * CHUNK)
        for c in range(tm // CHUNK):
            rows = slice(c * CHUNK, (c + 1) * CHUNK)
            s = _dot(wg, vb[rows, cols]) + bs_ref[:, cols]
            gout_ref[rows, cols] = (u[rows, cols] * s).astype(BF16)
    o = 2 * G_WIDTH
    qb_ref[...] = (_dot(xb, w_ref[:, o:o + SB_WIDTH]) * SB_SCALE).astype(BF16)
    k = _dot(xb, w_ref[:, o + SB_WIDTH:o + 2 * SB_WIDTH])
    _store_head_rows(k_ref, k, SB_HEADS)
    kb_ref[...] = k.astype(BF16)
    vv = _dot(xb, w_ref[:, o + 2 * SB_WIDTH:o + 3 * SB_WIDTH])
    _store_head_rows(vv_ref, vv, SB_HEADS)
    vvb_ref[...] = vv.astype(BF16)
    o = o + 3 * SB_WIDTH
    qm_ref[...] = (_dot(xb, w_ref[:, o:o + MEM_WIDTH]) * MEM_SCALE).astype(BF16)


def _proj(x2, w_a, lng, lnb, ws, bs_full, tm=256):
    n = x2.shape[0]
    row = lambda w: pl.BlockSpec((tm, w), lambda i: (i, 0))
    full = lambda a: pl.BlockSpec(a.shape, lambda i: (0,) * a.ndim)
    out_shape = (
        jax.ShapeDtypeStruct((n, G_WIDTH), BF16),
        jax.ShapeDtypeStruct((n, G_WIDTH), F32),
        jax.ShapeDtypeStruct((n, SB_WIDTH), BF16),
        jax.ShapeDtypeStruct((n * SB_HEADS, SB_HD), F32),
        jax.ShapeDtypeStruct((n, SB_WIDTH), BF16),
        jax.ShapeDtypeStruct((n * SB_HEADS, SB_HD), F32),
        jax.ShapeDtypeStruct((n, SB_WIDTH), BF16),
        jax.ShapeDtypeStruct((n, MEM_WIDTH), BF16),
    )
    return pl.pallas_call(
        _proj_kernel,
        grid=(n // tm,),
        in_specs=[row(D_MODEL), full(w_a), full(lng), full(lnb), full(ws), full(bs_full)],
        out_specs=tuple(pl.BlockSpec((tm * (s.shape[0] // n), s.shape[1]), lambda i: (i, 0)) for s in out_shape),
        out_shape=out_shape,
        compiler_params=_params("parallel"),
        name="proj",
    )(x2, w_a, lng, lnb, ws, bs_full)


def _mem_kv_kernel(x_ref, w_ref, k_ref, v_ref):
    y = _dot(x_ref[...].astype(BF16), w_ref[...])
    _store_head_rows(k_ref, y[:, :MEM_WIDTH], MEM_HEADS)
    _store_head_rows(v_ref, y[:, MEM_WIDTH:], MEM_HEADS)


def _mem_kv(mem2, w, tm=256):
    n = mem2.shape[0]
    out = jax.ShapeDtypeStruct((n * MEM_HEADS, MEM_HD), F32)
    spec = pl.BlockSpec((tm * MEM_HEADS, MEM_HD), lambda i: (i, 0))
    return pl.pallas_call(
        _mem_kv_kernel,
        grid=(n // tm,),
        in_specs=[pl.BlockSpec((tm, D_MODEL), lambda i: (i, 0)), pl.BlockSpec(w.shape, lambda i: (0, 0))],
        out_specs=(spec, spec),
        out_shape=(out, out),
        compiler_params=_params("parallel"),
        name="mem_kv",
    )(mem2, w)


def _sb_block(q, kb, vb, r_mat, bias, carry, causal):
    c = SB_CHUNK
    z = _dot_nt(q, kb) + bias
    sp = jnp.maximum(z, 0.0) + jnp.log(1.0 + jnp.exp(-jnp.abs(z)))
    if causal is not None:
        sp = jnp.where(causal, sp, 0.0)
    spb = sp.astype(BF16)
    nchunk = kb.shape[0] // c
    local = [_dot(spb[:, j * c:(j + 1) * c], r_mat) for j in range(nchunk)]
    pieces = [None] * nchunk
    for j in range(nchunk - 1, -1, -1):
        pieces[j] = jnp.exp(z[:, j * c:(j + 1) * c] - local[j] - carry)
        carry = carry + local[j][:, 0:1]
    a = pieces[0] if nchunk == 1 else jnp.concatenate(pieces, axis=1)
    if causal is not None:
        a = jnp.where(causal, a, 0.0)
    return _dot(a.astype(BF16), vb), carry


def _sb_prompt_kernel(bias_ref, q_ref, k_ref, v_ref, r_ref, o_ref):
    hg = pl.program_id(1)
    i = pl.program_id(2)
    t = SB_BLOCK
    r_mat = r_ref[...]
    row = lax.broadcasted_iota(jnp.int32, (t, t), 0)
    col = lax.broadcasted_iota(jnp.int32, (t, t), 1)
    heads = [(slice(j * SB_HD, (j + 1) * SB_HD), bias_ref[hg * SB_HP + j]) for j in range(SB_HP)]

    def visit(start, state, causal):
        out = []
        for (cols, bias), (acc, carry) in zip(heads, state):
            d, carry = _sb_block(q_ref[:, cols], k_ref[pl.ds(start, t), cols], v_ref[pl.ds(start, t), cols],
                                 r_mat, bias, carry, causal)
            out.append((acc + d, carry))
        return tuple(out)

    zero = (jnp.zeros((t, SB_HD), F32), jnp.zeros((t, 1), F32))
    state = visit(pl.multiple_of(i * t, t), (zero,) * SB_HP, col < row)
    def two_blocks(j, st):
        st = visit(pl.multiple_of((i - 1 - 2 * j) * t, t), st, None)
        return visit(pl.multiple_of((i - 2 - 2 * j) * t, t), st, None)

    state = lax.fori_loop(0, i // 2, two_blocks, state)
    state = lax.cond(i % 2 == 1, lambda st: visit(0, st, None), lambda st: st, state)
    for (cols, _), (acc, _) in zip(heads, state):
        o_ref[:, cols] = acc.astype(o_ref.dtype)


def _suffix_ones(n):
    j = lax.broadcasted_iota(jnp.int32, (n, n), 0)
    s = lax.broadcasted_iota(jnp.int32, (n, n), 1)
    return (j >= s).astype(BF16)


def _sb_prompt(qb, kb, vb, bias, batch, seq):
    t = SB_BLOCK
    nq = seq // t
    w = SB_HP * SB_HD
    return pl.pallas_call(
        _sb_prompt_kernel,
        grid_spec=pltpu.PrefetchScalarGridSpec(
            num_scalar_prefetch=1,
            grid=(batch, SB_HEADS // SB_HP, nq),
            in_specs=[
                pl.BlockSpec((t, w), lambda b, h, i, bias: (b * nq + i, h)),
                pl.BlockSpec((seq, w), lambda b, h, i, bias: (b, h)),
                pl.BlockSpec((seq, w), lambda b, h, i, bias: (b, h)),
                pl.BlockSpec((SB_CHUNK, SB_CHUNK), lambda b, h, i, bias: (0, 0)),
            ],
            out_specs=pl.BlockSpec((t, w), lambda b, h, i, bias: (b * nq + i, h)),
        ),
        out_shape=jax.ShapeDtypeStruct(qb.shape, BF16),
        compiler_params=_params("parallel", "parallel", "arbitrary"),
        name="sb_prompt",
    )(bias, qb, kb, vb, _suffix_ones(SB_CHUNK))


def _sb_sample_kernel(pt_ref, q_ref, kn_ref, vn_ref, *refs):
    npg = PAGES_PER_STEP
    k_refs = refs[:npg]
    v_refs = refs[npg:2 * npg]
    r_ref, bias_ref, o_ref, qbd_ref, acc_ref, carry_ref = refs[2 * npg:]
    s = pl.program_id(1)
    t = SB_CHUNK
    nrow = q_ref.shape[1] * SB_HEADS
    ntok = q_ref.shape[1]
    r_mat = r_ref[...]
    bias = bias_ref[...]

    def visit(kb, vb, causal):
        d, c = _sb_block(qbd_ref[...], kb, vb, r_mat, bias, carry_ref[:, 0:1], causal)
        acc_ref[...] += d
        carry_ref[...] = jnp.broadcast_to(c, carry_ref.shape)

    @pl.when(s == 0)
    def _():
        qrep = jnp.concatenate([q_ref[0]] * SB_HEADS, axis=0)
        rh = lax.broadcasted_iota(jnp.int32, qrep.shape, 0) // ntok
        ch = lax.broadcasted_iota(jnp.int32, qrep.shape, 1) // SB_HD
        qbd_ref[...] = jnp.where(rh == ch, qrep, 0.0).astype(BF16)
        acc_ref[...] = jnp.zeros_like(acc_ref)
        carry_ref[...] = jnp.zeros_like(carry_ref)
        tq = lax.broadcasted_iota(jnp.int32, (nrow, t), 0) % ntok
        tk = lax.broadcasted_iota(jnp.int32, (nrow, t), 1)
        pad = jnp.zeros((t - ntok, SB_WIDTH), F32)
        visit(jnp.concatenate([kn_ref[0], pad], axis=0).astype(BF16),
              jnp.concatenate([vn_ref[0], pad], axis=0).astype(BF16), tk < tq)

    def rows(page_ref):
        return jnp.concatenate([page_ref[pl.ds(h, PAGE, stride=SB_HEADS), :] for h in range(SB_HEADS)],
                               axis=1).astype(BF16)

    visit(jnp.concatenate([rows(r) for r in k_refs], axis=0), jnp.concatenate([rows(r) for r in v_refs], axis=0), None)

    @pl.when(s == pl.num_programs(1) - 1)
    def _():
        for h in range(SB_HEADS):
            cols = slice(h * SB_HD, (h + 1) * SB_HD)
            o_ref[0, :, cols] = acc_ref[h * ntok:(h + 1) * ntok, cols].astype(o_ref.dtype)


def _sb_sample(q3, kn3, vn3, cache_k, cache_v, layer, page_table, bias_rows):
    nseq, ntok, _ = q3.shape
    npages = page_table.shape[1]
    npg = PAGES_PER_STEP
    assert npages % npg == 0 and ntok <= SB_CHUNK
    nsteps = npages // npg
    nrow = ntok * SB_HEADS

    def page_spec(j):
        return pl.BlockSpec((None, None, PAGE * SB_HEADS, SB_HD),
                            lambda b, s, pt: (layer, pt[b, (nsteps - 1 - s) * npg + j], 0, 0))

    seq_spec = lambda r: pl.BlockSpec((1, r, SB_WIDTH), lambda b, s, pt: (b, 0, 0))
    return pl.pallas_call(
        _sb_sample_kernel,
        grid_spec=pltpu.PrefetchScalarGridSpec(
            num_scalar_prefetch=1,
            grid=(nseq, nsteps),
            in_specs=[seq_spec(ntok), seq_spec(ntok), seq_spec(ntok)]
            + [page_spec(j) for j in range(npg)] + [page_spec(j) for j in range(npg)]
            + [pl.BlockSpec((SB_CHUNK, SB_CHUNK), lambda b, s, pt: (0, 0)),
               pl.BlockSpec((nrow, 1), lambda b, s, pt: (0, 0))],
            out_specs=seq_spec(ntok),
            scratch_shapes=[pltpu.VMEM((nrow, SB_WIDTH), BF16),
                            pltpu.VMEM((nrow, SB_WIDTH), F32),
                            pltpu.VMEM((nrow, 128), F32)],
        ),
        out_shape=jax.ShapeDtypeStruct(q3.shape, F32),
        compiler_params=_params("parallel", "arbitrary"),
        name="sb_sample",
    )(page_table, q3, kn3, vn3, *([cache_k] * npg), *([cache_v] * npg), _suffix_ones(SB_CHUNK), bias_rows)


def _mem_attn_kernel(q_ref, k_ref, v_ref, o_ref):
    group = k_ref.shape[0]
    tm = q_ref.shape[0] // group
    for g in range(group):
        rows = slice(g * tm, (g + 1) * tm)
        for h in range(MEM_HEADS):
            cols = slice(h * MEM_HD, (h + 1) * MEM_HD)
            s = _dot_nt(q_ref[rows, cols].astype(BF16), _load_head_rows(k_ref.at[g], h, MEM_HEADS).astype(BF16))
            e = jnp.exp(s - jnp.max(s, axis=-1, keepdims=True))
            p = e / jnp.sum(e, axis=-1, keepdims=True)
            pv = _dot(p.astype(BF16), _load_head_rows(v_ref.at[g], h, MEM_HEADS).astype(BF16))
            o_ref[rows, cols] = pv.astype(o_ref.dtype)


def _mem_attn(qm, mk3, mv3, tm, group=1):
    n = qm.shape[0]
    nb = mk3.shape[0] // group
    nt = n // (nb * group * tm)
    assert mk3.shape[0] % group == 0 and (group == 1 or nt == 1)
    kv = pl.BlockSpec((group, N_MEM * MEM_HEADS, MEM_HD), lambda b, i: (b, 0, 0))
    qo = pl.BlockSpec((group * tm, MEM_WIDTH), lambda b, i: (b * nt + i, 0))
    return pl.pallas_call(
        _mem_attn_kernel,
        grid=(nb, nt),
        in_specs=[qo, kv, kv],
        out_specs=qo,
        out_shape=jax.ShapeDtypeStruct(qm.shape, qm.dtype),
        compiler_params=_params("parallel", "parallel"),
        name="mem_attn",
    )(qm, mk3, mv3)


def _merge_kernel(alpha, x_ref, g_ref, sb_ref, m_ref, wg_ref, bg_ref, wbg_ref, wbs_ref, wbm_ref, wo_ref,
                  lng_ref, lnb_ref, o_ref):
    x = x_ref[...]
    gates = jax.nn.sigmoid(_dot(x.astype(BF16), wg_ref[...]) + bg_ref[...])
    d = D_MODEL
    z = (gates[:, 0:d] * _dot(g_ref[...], wbg_ref[...])
         + gates[:, d:2 * d] * _dot(sb_ref[...], wbs_ref[...])
         + gates[:, 2 * d:3 * d] * _dot(m_ref[...], wbm_ref[...]))
    mix = _dot(z.astype(BF16), wo_ref[...])
    o_ref[...] = _layer_norm(alpha * x + mix, lng_ref[...], lnb_ref[...])


def _merge(alpha, x2, gout, sb, mo, w_gate, b_gate, w_bg, w_bs, w_bm, w_o, lng, lnb, tm=256):
    n = x2.shape[0]
    row = lambda a: pl.BlockSpec((tm, a.shape[1]), lambda i: (i, 0))
    full = lambda a: pl.BlockSpec(a.shape, lambda i: (0,) * a.ndim)
    ws = (w_gate, b_gate, w_bg, w_bs, w_bm, w_o, lng, lnb)
    return pl.pallas_call(
        functools.partial(_merge_kernel, alpha),
        grid=(n // tm,),
        in_specs=[row(x2), row(gout), row(sb), row(mo)] + [full(a) for a in ws],
        out_specs=row(x2),
        out_shape=jax.ShapeDtypeStruct(x2.shape, F32),
        compiler_params=_params("parallel"),
        name="merge",
    )(x2, gout, sb, mo, *ws)


def _cmpx(v, i, j):
    hi = jnp.maximum(v[i], v[j])
    lo = jnp.minimum(v[i], v[j])
    v[i], v[j] = hi, lo


def _sort16_desc(v):
    v = list(v)
    n = len(v)
    k = 2
    while k <= n:
        j = k // 2
        while j >= 1:
            for i in range(n):
                l = i ^ j
                if l > i:
                    if (i & k) == 0:
                        _cmpx(v, i, l)
                    else:
                        _cmpx(v, l, i)
            j //= 2
        k *= 2
    return v


def _merge_top16(a, b):
    n = len(a)
    c = []
    for i in range(n):
        j = n - 1 - i
        c.append(jnp.maximum(a[i], b[j]) if j < len(b) else a[i])
    j = n // 2
    while j >= 1:
        for i in range(n):
            if (i & j) == 0:
                _cmpx(c, i, i + j)
        j //= 2
    return c


def _top16_rows(s):
    g = _sort16_desc([s[8 * i:8 * i + 8, :] for i in range(N_KEYS // 8)])
    for shift in (4, 2, 1):
        g = _merge_top16(g, [pltpu.roll(x, shift, 0) for x in g])
    return g


def _tile_rows(a, reps):
    return jnp.concatenate([a] * reps, axis=0)


_PK_T1 = 0
_PK_T2 = PEER_TOPK
_PK_CNT = 2 * PEER_TOPK
_PK_INVZ = 3 * PEER_TOPK + 1
_PK_ROWS = 3 * PEER_TOPK + 2


def _peer_score_kernel(x_ref, wq_ref, sk_ref, xt_ref, nsel_ref, c_ref, rank_ref, p_ref, s1_s, s2_s, pk_ref):
    tm = x_ref.shape[0]
    xt = x_ref[...].T.astype(BF16)
    xt_ref[...] = xt
    k = PEER_TOPK
    sub = lax.broadcasted_iota(jnp.int32, (8, tm), 0)
    pk_ref[0:2 * k] = jnp.zeros((2 * k, 8, tm), F32)

    def scores(h, carry):
        qt = _dot(wq_ref[pl.ds(pl.multiple_of(h * 2 * PEER_DH, 2 * PEER_DH), 2 * PEER_DH), :], xt)
        s1 = _dot(sk_ref[2 * h], qt[:PEER_DH].astype(BF16))
        s2 = _dot(sk_ref[2 * h + 1], qt[PEER_DH:].astype(BF16))
        s1_s[h] = s1
        s2_s[h] = s2
        mine = sub == h
        for base, s in ((_PK_T1, s1), (_PK_T2, s2)):
            for a, t in enumerate(_top16_rows(s)):
                pk_ref[base + a] = jnp.where(mine, t, pk_ref[base + a])
        return carry

    lax.fori_loop(0, PEER_HEADS, scores, 0, unroll=2)

    t1 = [pk_ref[_PK_T1 + a] for a in range(k)]
    t2 = [pk_ref[_PK_T2 + a] for a in range(k)]
    top = [t1[0] + t2[b] for b in range(k)]
    for a in range(1, k):
        top = _merge_top16(top, [t1[a] + t2[b] for b in range(k // (a + 1))])
    tau = top[k - 1]
    m = t1[0] + t2[0]
    zsum = jnp.zeros_like(m)
    for a in range(k):
        cnt = jnp.zeros_like(m)
        for b in range(k // (a + 1)):
            c = t1[a] + t2[b]
            sel = c >= tau
            zsum = zsum + jnp.where(sel, jnp.exp(c - m), 0.0)
            cnt = cnt + jnp.where(sel, 1.0, 0.0)
        pk_ref[_PK_CNT + a] = cnt
    pk_ref[_PK_CNT + k] = jnp.zeros_like(m)
    pk_ref[_PK_INVZ] = 1.0 / zsum

    for h in range(PEER_HEADS):
        def row(i, cs):
            return jnp.broadcast_to(pk_ref[i, h:h + 1, cs], (N_KEYS, 128))

        for cc in range(tm // 128):
            cs = slice(cc * 128, (cc + 1) * 128)
            s2 = s2_s[h, :, cs]
            rank = jnp.zeros_like(s2)
            for a in range(k):
                rank = jnp.where(row(_PK_T2 + a, cs) > s2, float(a + 1), rank)
            rank_ref[h, :, cs] = rank.astype(BF16)
            p_ref[h, :, cs] = jnp.exp(s2 - row(_PK_T2, cs)).astype(BF16)
            s1 = s1_s[h, :, cs]
            nsel = row(_PK_CNT, cs)
            for a in range(k):
                nsel = jnp.where(row(_PK_T1 + a, cs) > s1, row(_PK_CNT + a + 1, cs), nsel)
            nsel_ref[h, :, cs] = nsel
            c_ref[h, :, cs] = jnp.exp(s1 - row(_PK_T1, cs)) * row(_PK_INVZ, cs)


def _peer_score(x1, wq_t, sk, tm):
    n = x1.shape[0]
    big = lambda dt: jax.ShapeDtypeStruct((PEER_HEADS, N_KEYS, n), dt)
    bspec = pl.BlockSpec((PEER_HEADS, N_KEYS, tm), lambda i: (0, 0, i))
    return pl.pallas_call(
        _peer_score_kernel,
        grid=(n // tm,),
        in_specs=[pl.BlockSpec((tm, D_MODEL), lambda i: (i, 0)),
                  pl.BlockSpec(wq_t.shape, lambda i: (0, 0)),
                  pl.BlockSpec(sk.shape, lambda i: (0, 0, 0))],
        out_specs=(pl.BlockSpec((D_MODEL, tm), lambda i: (0, i)), bspec, bspec, bspec, bspec),
        out_shape=(jax.ShapeDtypeStruct((D_MODEL, n), BF16), big(F32), big(F32), big(BF16), big(BF16)),
        scratch_shapes=[pltpu.VMEM((PEER_HEADS, N_KEYS, tm), F32),
                        pltpu.VMEM((PEER_HEADS, N_KEYS, tm), F32),
                        pltpu.VMEM((_PK_ROWS, 8, tm), F32)],
        compiler_params=_params("parallel"),
        name="peer_score",
    )(x1, wq_t, sk)


EXPERT_STEP = 2048
EXPERT_SUB = 256


def _peer_dense_kernel(alpha, x_ref, xt_ref, u_ref, vt_ref, nsel_ref, c_ref, rank_ref, p_ref,
                       lng_ref, lnb_ref, o_ref, acc_ref, act_ref, w_ref, rank_s, p_s):
    e = pl.program_id(1)
    tm = x_ref.shape[0]
    nhalf = EXPERT_SUB // N_KEYS
    nsub = EXPERT_STEP // EXPERT_SUB

    @pl.when(e == 0)
    def _():
        acc_ref[...] = jnp.zeros_like(acc_ref)
        rank_s[...] = rank_ref[...]
        p_s[...] = p_ref[...]

    def sel_row(ref, h, r, cs):
        tile = jnp.broadcast_to(ref[h, r:r + 1, cs], (16, 128)).astype(BF16)
        return jnp.concatenate([tile] * (N_KEYS // 16), axis=0)

    xt = xt_ref[...]
    for sub in range(nsub):
        rows = slice(sub * EXPERT_SUB, (sub + 1) * EXPERT_SUB)
        act_ref[rows, :] = jax.nn.gelu(_dot(u_ref[rows, :], xt).astype(BF16))
    for sub in range(nsub):
        for cc in range(tm // 128):
            cs = slice(cc * 128, (cc + 1) * 128)
            g = [jnp.zeros((N_KEYS, 128), BF16) for _ in range(nhalf)]
            for h in range(PEER_HEADS):
                rank = rank_s[h, :, cs]
                p = p_s[h, :, cs]
                for half in range(nhalf):
                    r = sub * nhalf + half
                    wsel = jnp.minimum(jnp.maximum(sel_row(nsel_ref, h, r, cs) - rank, 0), sel_row(c_ref, h, r, cs))
                    g[half] = g[half] + p * wsel
            for half in range(nhalf):
                er = slice(sub * EXPERT_SUB + half * N_KEYS, sub * EXPERT_SUB + (half + 1) * N_KEYS)
                w_ref[er, cs] = g[half] * act_ref[er, cs]
    acc_ref[...] += _dot(vt_ref[...], w_ref[...])

    @pl.when(e == pl.num_programs(1) - 1)
    def _():
        y = acc_ref[...].T
        o_ref[...] = _layer_norm(alpha * x_ref[...] + y, lng_ref[...], lnb_ref[...])


def _peer_dense(alpha, x1, xt, u_b, vt_b, nsel, c, rank, p, lng, lnb, tm):
    n = x1.shape[0]
    ne = u_b.shape[0] // EXPERT_STEP
    big = pl.BlockSpec((PEER_HEADS, N_KEYS, tm), lambda i, e: (0, 0, i))
    rowsel = pl.BlockSpec((PEER_HEADS, EXPERT_STEP // N_KEYS, tm), lambda i, e: (0, e, i))
    return pl.pallas_call(
        functools.partial(_peer_dense_kernel, alpha),
        grid=(n // tm, ne),
        in_specs=[pl.BlockSpec((tm, D_MODEL), lambda i, e: (i, 0)),
                  pl.BlockSpec((D_MODEL, tm), lambda i, e: (0, i)),
                  pl.BlockSpec((EXPERT_STEP, D_MODEL), lambda i, e: (e, 0)),
                  pl.BlockSpec((None, D_MODEL, EXPERT_STEP), lambda i, e: (e, 0, 0)),
                  rowsel, rowsel, big, big,
                  pl.BlockSpec(lng.shape, lambda i, e: (0, 0)),
                  pl.BlockSpec(lnb.shape, lambda i, e: (0, 0))],
        out_specs=pl.BlockSpec((tm, D_MODEL), lambda i, e: (i, 0)),
        out_shape=jax.ShapeDtypeStruct(x1.shape, F32),
        scratch_shapes=[pltpu.VMEM((D_MODEL, tm), F32),
                        pltpu.VMEM((EXPERT_STEP, tm), BF16),
                        pltpu.VMEM((EXPERT_STEP, tm), BF16),
                        pltpu.VMEM((PEER_HEADS, N_KEYS, tm), BF16),
                        pltpu.VMEM((PEER_HEADS, N_KEYS, tm), BF16)],
        compiler_params=_params("parallel", "arbitrary"),
        name="peer_dense",
    )(x1, xt, u_b, vt_b, nsel, c, rank, p, lng, lnb)


def _peer(alpha, x1, wq_t, sk, u_b, vt_b, lng, lnb, tm):
    xt, nsel, c, rank, p = _peer_score(x1, wq_t, sk, tm)
    return _peer_dense(alpha, x1, xt, u_b, vt_b, nsel, c, rank, p, lng, lnb, tm)


def kernel(x_prompt, x_sample, mem_prompt, cache_sb_k, cache_sb_v, page_table, cache_mem_k, cache_mem_v,
           w_in, b_gate, gmlp_ln_g, gmlp_ln_b, w_spatial, b_spatial, sb_bias, w_mem_kv, w_br_gmlp, w_br_sb,
           w_br_mem, w_out, ln1_g, ln1_b, peer_wq, peer_subkeys, peer_u, peer_v, ln2_g, ln2_b):
    depth = w_in.shape[0]
    alpha = float((2 * depth) ** 0.25)
    batch, seq, d = x_prompt.shape
    nseq, ntok, _ = x_sample.shape
    assert seq % SB_BLOCK == 0 and (nseq * ntok) % 256 == 0 and CHUNK % ntok == 0

    xp = x_prompt.reshape(batch * seq, d)
    xs = x_sample.reshape(nseq * ntok, d)
    outs = [[] for _ in range(7)]
    row2 = lambda a: a.reshape(1, -1)
    for l in range(depth):
        w_a = w_in[l][:, :A_WIDTH].astype(BF16)
        w_gate = w_in[l][:, A_WIDTH:].astype(BF16)
        bg = b_gate[l].reshape(1, N_BRANCH * D_MODEL)
        lng, lnb = row2(gmlp_ln_g[l]), row2(gmlp_ln_b[l])
        ws_p = w_spatial[l].astype(BF16)
        bs_p = jnp.repeat(b_spatial[l].T, CHUNK, axis=1)
        reps = CHUNK // ntok
        eye = jnp.eye(reps, dtype=F32)
        ws_s = jnp.einsum('ab,gij->gaibj', eye, w_spatial[l][:, :ntok, :ntok]).reshape(G_GROUPS, CHUNK, CHUNK).astype(BF16)
        bs_s = jnp.tile(bs_p[:ntok], (reps, 1))
        w_bg, w_bs, w_bm, w_o = (w.astype(BF16) for w in (w_br_gmlp[l], w_br_sb[l], w_br_mem[l], w_out[l]))
        l1g, l1b, l2g, l2b = row2(ln1_g[l]), row2(ln1_b[l]), row2(ln2_g[l]), row2(ln2_b[l])
        wq_t = peer_wq[l].T.astype(BF16)
        sk = peer_subkeys[l].reshape(PEER_HEADS * 2, N_KEYS, PEER_DH).astype(BF16)
        u_b = peer_u[l].astype(BF16)
        vt_b = peer_v[l].astype(BF16).reshape(-1, EXPERT_STEP, D_MODEL).transpose(0, 2, 1)
        bias = sb_bias[l].astype(F32)

        gout, _, qb, k, kb, vv, vvb, qm = _proj(xp, w_a, lng, lnb, ws_p, bs_p)
        mk, mv = _mem_kv(mem_prompt.reshape(-1, d), w_mem_kv[l].astype(BF16))
        mem_rows = lambda a: a.reshape(-1, N_MEM * MEM_HEADS, MEM_HD)
        mk3, mv3 = mem_rows(mk), mem_rows(mv)
        sb = _sb_prompt(qb, kb, vvb, bias, batch, seq)
        mo = _mem_attn(qm, mk3, mv3, tm=512)
        x1 = _merge(alpha, xp, gout, sb, mo, w_gate, bg, w_bg, w_bs, w_bm, w_o, l1g, l1b)
        xp = _peer(alpha, x1, wq_t, sk, u_b, vt_b, l2g, l2b, tm=512)
        outs[0].append(k.reshape(batch, seq, SB_HEADS, SB_HD))
        outs[1].append(vv.reshape(batch, seq, SB_HEADS, SB_HD))
        outs[2].append(mk.reshape(batch, N_MEM, MEM_HEADS, MEM_HD))
        outs[3].append(mv.reshape(batch, N_MEM, MEM_HEADS, MEM_HD))

        gout, v, qb, k, kb, vv, vvb, qm = _proj(xs, w_a, lng, lnb, ws_s, bs_s)
        bias_rows = jnp.repeat(bias, ntok).reshape(SB_HEADS * ntok, 1)
        seq_rows = lambda a: a.astype(F32).reshape(nseq, ntok, SB_WIDTH)
        pages = lambda c: c.reshape(c.shape[0], c.shape[1], PAGE * SB_HEADS, SB_HD)
        sb = _sb_sample(seq_rows(qb), seq_rows(kb), seq_rows(vvb), pages(cache_sb_k), pages(cache_sb_v), l,
                        page_table, bias_rows).reshape(nseq * ntok, SB_WIDTH).astype(BF16)
        mo = _mem_attn(qm.astype(F32), mem_rows(cache_mem_k[l]), mem_rows(cache_mem_v[l]), tm=ntok,
                       group=8).astype(BF16)
        x1 = _merge(alpha, xs, gout, sb, mo, w_gate, bg, w_bg, w_bs, w_bm, w_o, l1g, l1b)
        xs = _peer(alpha, x1, wq_t, sk, u_b, vt_b, l2g, l2b, tm=256)
        outs[4].append(k.reshape(nseq, ntok, SB_HEADS, SB_HD))
        outs[5].append(vv.reshape(nseq, ntok, SB_HEADS, SB_HD))
        outs[6].append(v.reshape(nseq, ntok, G_GROUPS, CHUNK))

    return (xp.reshape(batch, seq, d), xs.reshape(nseq, ntok, d)) + tuple(jnp.stack(o) for o in outs)
```

```python
import functools

import jax
import jax.numpy as jnp
from jax import lax
from jax.experimental import pallas as pl
from jax.experimental.pallas import tpu as pltpu

F32 = jnp.float32
BF16 = jnp.bfloat16

D_MODEL = 1024
G_WIDTH = 512
G_GROUPS = 4
CHUNK = 128
SB_HEADS = 8
SB_HD = 128
SB_WIDTH = 1024
MEM_HEADS = 4
MEM_HD = 128
MEM_WIDTH = 512
N_MEM = 256
N_BRANCH = 3
A_WIDTH = 2 * G_WIDTH + 3 * SB_WIDTH + MEM_WIDTH
PEER_HEADS = 8
PEER_DH = 128
N_KEYS = 128
PEER_TOPK = 16
PAGE = 128
LN_EPS = 1e-5
SB_SCALE = SB_HD ** -0.5
MEM_SCALE = MEM_HD ** -0.5

SB_CHUNK = 256
SB_BLOCK = 512
SB_HP = 4
PAGES_PER_STEP = 16
VMEM_LIMIT = 52 * 1024 * 1024


def _dot(a, b):
    return jnp.dot(a, b, preferred_element_type=F32)


def _dot_nt(a, b):
    return lax.dot_general(a, b, (((1,), (1,)), ((), ())), preferred_element_type=F32)


def _layer_norm(x, g, b):
    mu = jnp.mean(x, axis=-1, keepdims=True)
    xc = x - mu
    var = jnp.mean(xc * xc, axis=-1, keepdims=True)
    return xc * lax.rsqrt(var + LN_EPS) * g + b


def _params(*sem):
    return pltpu.CompilerParams(dimension_semantics=sem, vmem_limit_bytes=VMEM_LIMIT)


def _store_head_rows(ref, y, heads):
    rows, hd = y.shape[0], y.shape[1] // heads
    for h in range(heads):
        ref[pl.ds(h, rows, stride=heads), :] = y[:, h * hd:(h + 1) * hd]


def _load_head_rows(ref, h, heads):
    return ref[pl.ds(h, ref.shape[0] // heads, stride=heads), :]


def _proj_kernel(x_ref, w_ref, lng_ref, lnb_ref, ws_ref, bs_ref,
                 gout_ref, v_ref, qb_ref, k_ref, kb_ref, vv_ref, vvb_ref, qm_ref):
    tm = x_ref.shape[0]
    xb = x_ref[...].astype(BF16)
    u = jax.nn.gelu(_dot(xb, w_ref[:, 0:G_WIDTH]))
    gv = jax.nn.gelu(_dot(xb, w_ref[:, G_WIDTH:2 * G_WIDTH]))
    v = _layer_norm(gv, lng_ref[...], lnb_ref[...])
    v_ref[...] = v
    vb = v.astype(BF16)
    row = lax.broadcasted_iota(jnp.int32, (CHUNK, CHUNK), 0)
    col = lax.broadcasted_iota(jnp.int32, (CHUNK, CHUNK), 1)
    tril = col <= row
    for g in range(G_GROUPS):
        wg = jnp.where(tril, ws_ref[g], jnp.zeros((), BF16))
        cols = slice(g * CHUNK, (g + 1) * CHUNK)
        for c in range(tm // CHUNK):
            rows = slice(c * CHUNK, (c + 1) * CHUNK)
            s = _dot(wg, vb[rows, cols]) + bs_ref[:, cols]
            gout_ref[rows, cols] = (u[rows, cols] * s).astype(BF16)
    o = 2 * G_WIDTH
    qb_ref[...] = (_dot(xb, w_ref[:, o:o + SB_WIDTH]) * SB_SCALE).astype(BF16)
    k = _dot(xb, w_ref[:, o + SB_WIDTH:o + 2 * SB_WIDTH])
    _store_head_rows(k_ref, k, SB_HEADS)
    kb_ref[...] = k.astype(BF16)
    vv = _dot(xb, w_ref[:, o + 2 * SB_WIDTH:o + 3 * SB_WIDTH])
    _store_head_rows(vv_ref, vv, SB_HEADS)
    vvb_ref[...] = vv.astype(BF16)
    o = o + 3 * SB_WIDTH
    qm_ref[...] = (_dot(xb, w_ref[:, o:o + MEM_WIDTH]) * MEM_SCALE).astype(BF16)


def _proj(x2, w_a, lng, lnb, ws, bs_full, tm=256):
    n = x2.shape[0]
    row = lambda w: pl.BlockSpec((tm, w), lambda i: (i, 0))
    full = lambda a: pl.BlockSpec(a.shape, lambda i: (0,) * a.ndim)
    out_shape = (
        jax.ShapeDtypeStruct((n, G_WIDTH), BF16),
        jax.ShapeDtypeStruct((n, G_WIDTH), F32),
        jax.ShapeDtypeStruct((n, SB_WIDTH), BF16),
        jax.ShapeDtypeStruct((n * SB_HEADS, SB_HD), F32),
        jax.ShapeDtypeStruct((n, SB_WIDTH), BF16),
        jax.ShapeDtypeStruct((n * SB_HEADS, SB_HD), F32),
        jax.ShapeDtypeStruct((n, SB_WIDTH), BF16),
        jax.ShapeDtypeStruct((n, MEM_WIDTH), BF16),
    )
    return pl.pallas_call(
        _proj_kernel,
        grid=(n // tm,),
        in_specs=[row(D_MODEL), full(w_a), full(lng), full(lnb), full(ws), full(bs_full)],
        out_specs=tuple(pl.BlockSpec((tm * (s.shape[0] // n), s.shape[1]), lambda i: (i, 0)) for s in out_shape),
        out_shape=out_shape,
        compiler_params=_params("parallel"),
        name="proj",
    )(x2, w_a, lng, lnb, ws, bs_full)


def _mem_kv_kernel(x_ref, w_ref, k_ref, v_ref):
    y = _dot(x_ref[...].astype(BF16), w_ref[...])
    _store_head_rows(k_ref, y[:, :MEM_WIDTH], MEM_HEADS)
    _store_head_rows(v_ref, y[:, MEM_WIDTH:], MEM_HEADS)


def _mem_kv(mem2, w, tm=256):
    n = mem2.shape[0]
    out = jax.ShapeDtypeStruct((n * MEM_HEADS, MEM_HD), F32)
    spec = pl.BlockSpec((tm * MEM_HEADS, MEM_HD), lambda i: (i, 0))
    return pl.pallas_call(
        _mem_kv_kernel,
        grid=(n // tm,),
        in_specs=[pl.BlockSpec((tm, D_MODEL), lambda i: (i, 0)), pl.BlockSpec(w.shape, lambda i: (0, 0))],
        out_specs=(spec, spec),
        out_shape=(out, out),
        compiler_params=_params("parallel"),
        name="mem_kv",
    )(mem2, w)


def _sb_block(q, kb, vb, r_mat, bias, carry, causal):
    c = SB_CHUNK
    z = _dot_nt(q, kb) + bias
    sp = jnp.maximum(z, 0.0) + jnp.log(1.0 + jnp.exp(-jnp.abs(z)))
    if causal is not None:
        sp = jnp.where(causal, sp, 0.0)
    spb = sp.astype(BF16)
    nchunk = kb.shape[0] // c
    local = [_dot(spb[:, j * c:(j + 1) * c], r_mat) for j in range(nchunk)]
    pieces = [None] * nchunk
    for j in range(nchunk - 1, -1, -1):
        pieces[j] = jnp.exp(z[:, j * c:(j + 1) * c] - local[j] - carry)
        carry = carry + local[j][:, 0:1]
    a = pieces[0] if nchunk == 1 else jnp.concatenate(pieces, axis=1)
    if causal is not None:
        a = jnp.where(causal, a, 0.0)
    return _dot(a.astype(BF16), vb), carry


def _sb_prompt_kernel(bias_ref, q_ref, k_ref, v_ref, r_ref, o_ref):
    hg = pl.program_id(1)
    i = pl.program_id(2)
    t = SB_BLOCK
    r_mat = r_ref[...]
    row = lax.broadcasted_iota(jnp.int32, (t, t), 0)
    col = lax.broadcasted_iota(jnp.int32, (t, t), 1)
    heads = [(slice(j * SB_HD, (j + 1) * SB_HD), bias_ref[hg * SB_HP + j]) for j in range(SB_HP)]

    def visit(start, state, causal):
        out = []
        for (cols, bias), (acc, carry) in zip(heads, state):
            d, carry = _sb_block(q_ref[:, cols], k_ref[pl.ds(start, t), cols], v_ref[pl.ds(start, t), cols],
                                 r_mat, bias, carry, causal)
            out.append((acc + d, carry))
        return tuple(out)

    zero = (jnp.zeros((t, SB_HD), F32), jnp.zeros((t, 1), F32))
    state = visit(pl.multiple_of(i * t, t), (zero,) * SB_HP, col < row)
    def two_blocks(j, st):
        st = visit(pl.multiple_of((i - 1 - 2 * j) * t, t), st, None)
        return visit(pl.multiple_of((i - 2 - 2 * j) * t, t), st, None)

    state = lax.fori_loop(0, i // 2, two_blocks, state)
    state = lax.cond(i % 2 == 1, lambda st: visit(0, st, None), lambda st: st, state)
    for (cols, _), (acc, _) in zip(heads, state):
        o_ref[:, cols] = acc.astype(o_ref.dtype)


def _suffix_ones(n):
    j = lax.broadcasted_iota(jnp.int32, (n, n), 0)
    s = lax.broadcasted_iota(jnp.int32, (n, n), 1)
    return (j >= s).astype(BF16)


def _sb_prompt(qb, kb, vb, bias, batch, seq):
    t = SB_BLOCK
    nq = seq // t
    w = SB_HP * SB_HD
    return pl.pallas_call(
        _sb_prompt_kernel,
        grid_spec=pltpu.PrefetchScalarGridSpec(
            num_scalar_prefetch=1,
            grid=(batch, SB_HEADS // SB_HP, nq),
            in_specs=[
                pl.BlockSpec((t, w), lambda b, h, i, bias: (b * nq + i, h)),
                pl.BlockSpec((seq, w), lambda b, h, i, bias: (b, h)),
                pl.BlockSpec((seq, w), lambda b, h, i, bias: (b, h)),
                pl.BlockSpec((SB_CHUNK, SB_CHUNK), lambda b, h, i, bias: (0, 0)),
            ],
            out_specs=pl.BlockSpec((t, w), lambda b, h, i, bias: (b * nq + i, h)),
        ),
        out_shape=jax.ShapeDtypeStruct(qb.shape, BF16),
        compiler_params=_params("parallel", "parallel", "arbitrary"),
        name="sb_prompt",
    )(bias, qb, kb, vb, _suffix_ones(SB_CHUNK))


def _sb_sample_kernel(pt_ref, q_ref, kn_ref, vn_ref, *refs):
    npg = PAGES_PER_STEP
    k_refs = refs[:npg]
    v_refs = refs[npg:2 * npg]
    r_ref, bias_ref, o_ref, qbd_ref, acc_ref, carry_ref = refs[2 * npg:]
    s = pl.program_id(1)
    t = SB_CHUNK
    nrow = q_ref.shape[1] * SB_HEADS
    ntok = q_ref.shape[1]
    r_mat = r_ref[...]
    bias = bias_ref[...]

    def visit(kb, vb, causal):
        d, c = _sb_block(qbd_ref[...], kb, vb, r_mat, bias, carry_ref[:, 0:1], causal)
        acc_ref[...] += d
        carry_ref[...] = jnp.broadcast_to(c, carry_ref.shape)

    @pl.when(s == 0)
    def _():
        qrep = jnp.concatenate([q_ref[0]] * SB_HEADS, axis=0)
        rh = lax.broadcasted_iota(jnp.int32, qrep.shape, 0) // ntok
        ch = lax.broadcasted_iota(jnp.int32, qrep.shape, 1) // SB_HD
        qbd_ref[...] = jnp.where(rh == ch, qrep, 0.0).astype(BF16)
        acc_ref[...] = jnp.zeros_like(acc_ref)
        carry_ref[...] = jnp.zeros_like(carry_ref)
        tq = lax.broadcasted_iota(jnp.int32, (nrow, t), 0) % ntok
        tk = lax.broadcasted_iota(jnp.int32, (nrow, t), 1)
        pad = jnp.zeros((t - ntok, SB_WIDTH), F32)
        visit(jnp.concatenate([kn_ref[0], pad], axis=0).astype(BF16),
              jnp.concatenate([vn_ref[0], pad], axis=0).astype(BF16), tk < tq)

    def rows(page_ref):
        return jnp.concatenate([page_ref[pl.ds(h, PAGE, stride=SB_HEADS), :] for h in range(SB_HEADS)],
                               axis=1).astype(BF16)

    visit(jnp.concatenate([rows(r) for r in k_refs], axis=0), jnp.concatenate([rows(r) for r in v_refs], axis=0), None)

    @pl.when(s == pl.num_programs(1) - 1)
    def _():
        for h in range(SB_HEADS):
            cols = slice(h * SB_HD, (h + 1) * SB_HD)
            o_ref[0, :, cols] = acc_ref[h * ntok:(h + 1) * ntok, cols].astype(o_ref.dtype)


def _sb_sample(q3, kn3, vn3, cache_k, cache_v, layer, page_table, bias_rows):
    nseq, ntok, _ = q3.shape
    npages = page_table.shape[1]
    npg = PAGES_PER_STEP
    assert npages % npg == 0 and ntok <= SB_CHUNK
    nsteps = npages // npg
    nrow = ntok * SB_HEADS

    def page_spec(j):
        return pl.BlockSpec((None, None, PAGE * SB_HEADS, SB_HD),
                            lambda b, s, pt: (layer, pt[b, (nsteps - 1 - s) * npg + j], 0, 0))

    seq_spec = lambda r: pl.BlockSpec((1, r, SB_WIDTH), lambda b, s, pt: (b, 0, 0))
    return pl.pallas_call(
        _sb_sample_kernel,
        grid_spec=pltpu.PrefetchScalarGridSpec(
            num_scalar_prefetch=1,
            grid=(nseq, nsteps),
            in_specs=[seq_spec(ntok), seq_spec(ntok), seq_spec(ntok)]
            + [page_spec(j) for j in range(npg)] + [page_spec(j) for j in range(npg)]
            + [pl.BlockSpec((SB_CHUNK, SB_CHUNK), lambda b, s, pt: (0, 0)),
               pl.BlockSpec((nrow, 1), lambda b, s, pt: (0, 0))],
            out_specs=seq_spec(ntok),
            scratch_shapes=[pltpu.VMEM((nrow, SB_WIDTH), BF16),
                            pltpu.VMEM((nrow, SB_WIDTH), F32),
                            pltpu.VMEM((nrow, 128), F32)],
        ),
        out_shape=jax.ShapeDtypeStruct(q3.shape, F32),
        compiler_params=_params("parallel", "arbitrary"),
        name="sb_sample",
    )(page_table, q3, kn3, vn3, *([cache_k] * npg), *([cache_v] * npg), _suffix_ones(SB_CHUNK), bias_rows)


def _mem_attn_kernel(q_ref, k_ref, v_ref, o_ref):
    group = k_ref.shape[0]
    tm = q_ref.shape[0] // group
    for g in range(group):
        rows = slice(g * tm, (g + 1) * tm)
        for h in range(MEM_HEADS):
            cols = slice(h * MEM_HD, (h + 1) * MEM_HD)
            s = _dot_nt(q_ref[rows, cols].astype(BF16), _load_head_rows(k_ref.at[g], h, MEM_HEADS).astype(BF16))
            e = jnp.exp(s - jnp.max(s, axis=-1, keepdims=True))
            p = e / jnp.sum(e, axis=-1, keepdims=True)
            pv = _dot(p.astype(BF16), _load_head_rows(v_ref.at[g], h, MEM_HEADS).astype(BF16))
            o_ref[rows, cols] = pv.astype(o_ref.dtype)


def _mem_attn(qm, mk3, mv3, tm, group=1):
    n = qm.shape[0]
    nb = mk3.shape[0] // group
    nt = n // (nb * group * tm)
    assert mk3.shape[0] % group == 0 and (group == 1 or nt == 1)
    kv = pl.BlockSpec((group, N_MEM * MEM_HEADS, MEM_HD), lambda b, i: (b, 0, 0))
    qo = pl.BlockSpec((group * tm, MEM_WIDTH), lambda b, i: (b * nt + i, 0))
    return pl.pallas_call(
        _mem_attn_kernel,
        grid=(nb, nt),
        in_specs=[qo, kv, kv],
        out_specs=qo,
        out_shape=jax.ShapeDtypeStruct(qm.shape, qm.dtype),
        compiler_params=_params("parallel", "parallel"),
        name="mem_attn",
    )(qm, mk3, mv3)


def _merge_kernel(alpha, x_ref, g_ref, sb_ref, m_ref, wg_ref, bg_ref, wbg_ref, wbs_ref, wbm_ref, wo_ref,
                  lng_ref, lnb_ref, o_ref):
    x = x_ref[...]
    gates = jax.nn.sigmoid(_dot(x.astype(BF16), wg_ref[...]) + bg_ref[...])
    d = D_MODEL
    z = (gates[:, 0:d] * _dot(g_ref[...], wbg_ref[...])
         + gates[:, d:2 * d] * _dot(sb_ref[...], wbs_ref[...])
         + gates[:, 2 * d:3 * d] * _dot(m_ref[...], wbm_ref[...]))
    mix = _dot(z.astype(BF16), wo_ref[...])
    o_ref[...] = _layer_norm(alpha * x + mix, lng_ref[...], lnb_ref[...])


def _merge(alpha, x2, gout, sb, mo, w_gate, b_gate, w_bg, w_bs, w_bm, w_o, lng, lnb, tm=256):
    n = x2.shape[0]
    row = lambda a: pl.BlockSpec((tm, a.shape[1]), lambda i: (i, 0))
    full = lambda a: pl.BlockSpec(a.shape, lambda i: (0,) * a.ndim)
    ws = (w_gate, b_gate, w_bg, w_bs, w_bm, w_o, lng, lnb)
    return pl.pallas_call(
        functools.partial(_merge_kernel, alpha),
        grid=(n // tm,),
        in_specs=[row(x2), row(gout), row(sb), row(mo)] + [full(a) for a in ws],
        out_specs=row(x2),
        out_shape=jax.ShapeDtypeStruct(x2.shape, F32),
        compiler_params=_params("parallel"),
        name="merge",
    )(x2, gout, sb, mo, *ws)


def _cmpx(v, i, j):
    hi = jnp.maximum(v[i], v[j])
    lo = jnp.minimum(v[i], v[j])
    v[i], v[j] = hi, lo


def _sort16_desc(v):
    v = list(v)
    n = len(v)
    k = 2
    while k <= n:
        j = k // 2
        while j >= 1:
            for i in range(n):
                l = i ^ j
                if l > i:
                    if (i & k) == 0:
                        _cmpx(v, i, l)
                    else:
                        _cmpx(v, l, i)
            j //= 2
        k *= 2
    return v


def _merge_top16(a, b):
    n = len(a)
    c = []
    for i in range(n):
        j = n - 1 - i
        c.append(jnp.maximum(a[i], b[j]) if j < len(b) else a[i])
    j = n // 2
    while j >= 1:
        for i in range(n):
            if (i & j) == 0:
                _cmpx(c, i, i + j)
        j //= 2
    return c


def _top16_rows(s):
    g = _sort16_desc([s[8 * i:8 * i + 8, :] for i in range(N_KEYS // 8)])
    for shift in (4, 2, 1):
        g = _merge_top16(g, [pltpu.roll(x, shift, 0) for x in g])
    return g


def _tile_rows(a, reps):
    return jnp.concatenate([a] * reps, axis=0)


_PK_T1 = 0
_PK_T2 = PEER_TOPK
_PK_CNT = 2 * PEER_TOPK
_PK_INVZ = 3 * PEER_TOPK + 1
_PK_ROWS = 3 * PEER_TOPK + 2


def _peer_score_kernel(x_ref, wq_ref, sk_ref, xt_ref, nsel_ref, c_ref, rank_ref, p_ref, s1_s, s2_s, pk_ref):
    tm = x_ref.shape[0]
    xt = x_ref[...].T.astype(BF16)
    xt_ref[...] = xt
    k = PEER_TOPK
    sub = lax.broadcasted_iota(jnp.int32, (8, tm), 0)
    pk_ref[0:2 * k] = jnp.zeros((2 * k, 8, tm), F32)

    def scores(h, carry):
        qt = _dot(wq_ref[pl.ds(pl.multiple_of(h * 2 * PEER_DH, 2 * PEER_DH), 2 * PEER_DH), :], xt)
        s1 = _dot(sk_ref[2 * h], qt[:PEER_DH].astype(BF16))
        s2 = _dot(sk_ref[2 * h + 1], qt[PEER_DH:].astype(BF16))
        s1_s[h] = s1
        s2_s[h] = s2
        mine = sub == h
        for base, s in ((_PK_T1, s1), (_PK_T2, s2)):
            for a, t in enumerate(_top16_rows(s)):
                pk_ref[base + a] = jnp.where(mine, t, pk_ref[base + a])
        return carry

    lax.fori_loop(0, PEER_HEADS, scores, 0, unroll=2)

    t1 = [pk_ref[_PK_T1 + a] for a in range(k)]
    t2 = [pk_ref[_PK_T2 + a] for a in range(k)]
    top = [t1[0] + t2[b] for b in range(k)]
    for a in range(1, k):
        top = _merge_top16(top, [t1[a] + t2[b] for b in range(k // (a + 1))])
    tau = top[k - 1]
    m = t1[0] + t2[0]
    zsum = jnp.zeros_like(m)
    for a in range(k):
        cnt = jnp.zeros_like(m)
        for b in range(k // (a + 1)):
            c = t1[a] + t2[b]
            sel = c >= tau
            zsum = zsum + jnp.where(sel, jnp.exp(c - m), 0.0)
            cnt = cnt + jnp.where(sel, 1.0, 0.0)
        pk_ref[_PK_CNT + a] = cnt
    pk_ref[_PK_CNT + k] = jnp.zeros_like(m)
    pk_ref[_PK_INVZ] = 1.0 / zsum

    for h in range(PEER_HEADS):
        def row(i, cs):
            return jnp.broadcast_to(pk_ref[i, h:h + 1, cs], (N_KEYS, 128))

        for cc in range(tm // 128):
            cs = slice(cc * 128, (cc + 1) * 128)
            s2 = s2_s[h, :, cs]
            rank = jnp.zeros_like(s2)
            for a in range(k):
                rank = jnp.where(row(_PK_T2 + a, cs) > s2, float(a + 1), rank)
            rank_ref[h, :, cs] = rank.astype(BF16)
            p_ref[h, :, cs] = jnp.exp(s2 - row(_PK_T2, cs)).astype(BF16)
            s1 = s1_s[h, :, cs]
            nsel = row(_PK_CNT, cs)
            for a in range(k):
                nsel = jnp.where(row(_PK_T1 + a, cs) > s1, row(_PK_CNT + a + 1, cs), nsel)
            nsel_ref[h, :, cs] = nsel
            c_ref[h, :, cs] = jnp.exp(s1 - row(_PK_T1, cs)) * row(_PK_INVZ, cs)


def _peer_score(x1, wq_t, sk, tm):
    n = x1.shape[0]
    big = lambda dt: jax.ShapeDtypeStruct((PEER_HEADS, N_KEYS, n), dt)
    bspec = pl.BlockSpec((PEER_HEADS, N_KEYS, tm), lambda i: (0, 0, i))
    return pl.pallas_call(
        _peer_score_kernel,
        grid=(n // tm,),
        in_specs=[pl.BlockSpec((tm, D_MODEL), lambda i: (i, 0)),
                  pl.BlockSpec(wq_t.shape, lambda i: (0, 0)),
                  pl.BlockSpec(sk.shape, lambda i: (0, 0, 0))],
        out_specs=(pl.BlockSpec((D_MODEL, tm), lambda i: (0, i)), bspec, bspec, bspec, bspec),
        out_shape=(jax.ShapeDtypeStruct((D_MODEL, n), BF16), big(F32), big(F32), big(BF16), big(BF16)),
        scratch_shapes=[pltpu.VMEM((PEER_HEADS, N_KEYS, tm), F32),
                        pltpu.VMEM((PEER_HEADS, N_KEYS, tm), F32),
                        pltpu.VMEM((_PK_ROWS, 8, tm), F32)],
        compiler_params=_params("parallel"),
        name="peer_score",
    )(x1, wq_t, sk)


EXPERT_STEP = 2048
EXPERT_SUB = 256


def _peer_dense_kernel(alpha, x_ref, xt_ref, u_ref, vt_ref, nsel_ref, c_ref, rank_ref, p_ref,
                       lng_ref, lnb_ref, o_ref, acc_ref, act_ref, w_ref, rank_s, p_s):
    e = pl.program_id(1)
    tm = x_ref.shape[0]
    nhalf = EXPERT_SUB // N_KEYS
    nsub = EXPERT_STEP // EXPERT_SUB

    @pl.when(e == 0)
    def _():
        acc_ref[...] = jnp.zeros_like(acc_ref)
        rank_s[...] = rank_ref[...]
        p_s[...] = p_ref[...]

    def sel_row(ref, h, r, cs):
        tile = jnp.broadcast_to(ref[h, r:r + 1, cs], (16, 128)).astype(BF16)
        return jnp.concatenate([tile] * (N_KEYS // 16), axis=0)

    xt = xt_ref[...]
    for sub in range(nsub):
        rows = slice(sub * EXPERT_SUB, (sub + 1) * EXPERT_SUB)
        act_ref[rows, :] = jax.nn.gelu(_dot(u_ref[rows, :], xt).astype(BF16))
    for sub in range(nsub):
        for cc in range(tm // 128):
            cs = slice(cc * 128, (cc + 1) * 128)
            g = [jnp.zeros((N_KEYS, 128), BF16) for _ in range(nhalf)]
            for h in range(PEER_HEADS):
                rank = rank_s[h, :, cs]
                p = p_s[h, :, cs]
                for half in range(nhalf):
                    r = sub * nhalf + half
                    wsel = jnp.minimum(jnp.maximum(sel_row(nsel_ref, h, r, cs) - rank, 0), sel_row(c_ref, h, r, cs))
                    g[half] = g[half] + p * wsel
            for half in range(nhalf):
                er = slice(sub * EXPERT_SUB + half * N_KEYS, sub * EXPERT_SUB + (half + 1) * N_KEYS)
                w_ref[er, cs] = g[half] * act_ref[er, cs]
    acc_ref[...] += _dot(vt_ref[...], w_ref[...])

    @pl.when(e == pl.num_programs(1) - 1)
    def _():
        y = acc_ref[...].T
        o_ref[...] = _layer_norm(alpha * x_ref[...] + y, lng_ref[...], lnb_ref[...])


def _peer_dense(alpha, x1, xt, u_b, vt_b, nsel, c, rank, p, lng, lnb, tm):
    n = x1.shape[0]
    ne = u_b.shape[0] // EXPERT_STEP
    big = pl.BlockSpec((PEER_HEADS, N_KEYS, tm), lambda i, e: (0, 0, i))
    rowsel = pl.BlockSpec((PEER_HEADS, EXPERT_STEP // N_KEYS, tm), lambda i, e: (0, e, i))
    return pl.pallas_call(
        functools.partial(_peer_dense_kernel, alpha),
        grid=(n // tm, ne),
        in_specs=[pl.BlockSpec((tm, D_MODEL), lambda i, e: (i, 0)),
                  pl.BlockSpec((D_MODEL, tm), lambda i, e: (0, i)),
                  pl.BlockSpec((EXPERT_STEP, D_MODEL), lambda i, e: (e, 0)),
                  pl.BlockSpec((None, D_MODEL, EXPERT_STEP), lambda i, e: (e, 0, 0)),
                  rowsel, rowsel, big, big,
                  pl.BlockSpec(lng.shape, lambda i, e: (0, 0)),
                  pl.BlockSpec(lnb.shape, lambda i, e: (0, 0))],
        out_specs=pl.BlockSpec((tm, D_MODEL), lambda i, e: (i, 0)),
        out_shape=jax.ShapeDtypeStruct(x1.shape, F32),
        scratch_shapes=[pltpu.VMEM((D_MODEL, tm), F32),
                        pltpu.VMEM((EXPERT_STEP, tm), BF16),
                        pltpu.VMEM((EXPERT_STEP, tm), BF16),
                        pltpu.VMEM((PEER_HEADS, N_KEYS, tm), BF16),
                        pltpu.VMEM((PEER_HEADS, N_KEYS, tm), BF16)],
        compiler_params=_params("parallel", "arbitrary"),
        name="peer_dense",
    )(x1, xt, u_b, vt_b, nsel, c, rank, p, lng, lnb)


def _peer(alpha, x1, wq_t, sk, u_b, vt_b, lng, lnb, tm):
    xt, nsel, c, rank, p = _peer_score(x1, wq_t, sk, tm)
    return _peer_dense(alpha, x1, xt, u_b, vt_b, nsel, c, rank, p, lng, lnb, tm)


def kernel(x_prompt, x_sample, mem_prompt, cache_sb_k, cache_sb_v, page_table, cache_mem_k, cache_mem_v,
           w_in, b_gate, gmlp_ln_g, gmlp_ln_b, w_spatial, b_spatial, sb_bias, w_mem_kv, w_br_gmlp, w_br_sb,
           w_br_mem, w_out, ln1_g, ln1_b, peer_wq, peer_subkeys, peer_u, peer_v, ln2_g, ln2_b):
    depth = w_in.shape[0]
    alpha = float((2 * depth) ** 0.25)
    batch, seq, d = x_prompt.shape
    nseq, ntok, _ = x_sample.shape
    assert seq % SB_BLOCK == 0 and (nseq * ntok) % 256 == 0 and CHUNK % ntok == 0

    xp = x_prompt.reshape(batch * seq, d)
    xs = x_sample.reshape(nseq * ntok, d)
    outs = [[] for _ in range(7)]
    row2 = lambda a: a.reshape(1, -1)
    for l in range(depth):
        w_a = w_in[l][:, :A_WIDTH].astype(BF16)
        w_gate = w_in[l][:, A_WIDTH:].astype(BF16)
        bg = b_gate[l].reshape(1, N_BRANCH * D_MODEL)
        lng, lnb = row2(gmlp_ln_g[l]), row2(gmlp_ln_b[l])
        ws_p = w_spatial[l].astype(BF16)
        bs_p = jnp.repeat(b_spatial[l].T, CHUNK, axis=1)
        reps = CHUNK // ntok
        eye = jnp.eye(reps, dtype=F32)
        ws_s = jnp.einsum('ab,gij->gaibj', eye, w_spatial[l][:, :ntok, :ntok]).reshape(G_GROUPS, CHUNK, CHUNK).astype(BF16)
        bs_s = jnp.tile(bs_p[:ntok], (reps, 1))
        w_bg, w_bs, w_bm, w_o = (w.astype(BF16) for w in (w_br_gmlp[l], w_br_sb[l], w_br_mem[l], w_out[l]))
        l1g, l1b, l2g, l2b = row2(ln1_g[l]), row2(ln1_b[l]), row2(ln2_g[l]), row2(ln2_b[l])
        wq_t = peer_wq[l].T.astype(BF16)
        sk = peer_subkeys[l].reshape(PEER_HEADS * 2, N_KEYS, PEER_DH).astype(BF16)
        u_b = peer_u[l].astype(BF16)
        vt_b = peer_v[l].astype(BF16).reshape(-1, EXPERT_STEP, D_MODEL).transpose(0, 2, 1)
        bias = sb_bias[l].astype(F32)

        gout, _, qb, k, kb, vv, vvb, qm = _proj(xp, w_a, lng, lnb, ws_p, bs_p)
        mk, mv = _mem_kv(mem_prompt.reshape(-1, d), w_mem_kv[l].astype(BF16))
        mem_rows = lambda a: a.reshape(-1, N_MEM * MEM_HEADS, MEM_HD)
        mk3, mv3 = mem_rows(mk), mem_rows(mv)
        sb = _sb_prompt(qb, kb, vvb, bias, batch, seq)
        mo = _mem_attn(qm, mk3, mv3, tm=512)
        x1 = _merge(alpha, xp, gout, sb, mo, w_gate, bg, w_bg, w_bs, w_bm, w_o, l1g, l1b)
        xp = _peer(alpha, x1, wq_t, sk, u_b, vt_b, l2g, l2b, tm=512)
        outs[0].append(k.reshape(batch, seq, SB_HEADS, SB_HD))
        outs[1].append(vv.reshape(batch, seq, SB_HEADS, SB_HD))
        outs[2].append(mk.reshape(batch, N_MEM, MEM_HEADS, MEM_HD))
        outs[3].append(mv.reshape(batch, N_MEM, MEM_HEADS, MEM_HD))

        gout, v, qb, k, kb, vv, vvb, qm = _proj(xs, w_a, lng, lnb, ws_s, bs_s)
        bias_rows = jnp.repeat(bias, ntok).reshape(SB_HEADS * ntok, 1)
        seq_rows = lambda a: a.astype(F32).reshape(nseq, ntok, SB_WIDTH)
        pages = lambda c: c.reshape(c.shape[0], c.shape[1], PAGE * SB_HEADS, SB_HD)
        sb = _sb_sample(seq_rows(qb), seq_rows(kb), seq_rows(vvb), pages(cache_sb_k), pages(cache_sb_v), l,
                        page_table, bias_rows).reshape(nseq * ntok, SB_WIDTH).astype(BF16)
        mo = _mem_attn(qm.astype(F32), mem_rows(cache_mem_k[l]), mem_rows(cache_mem_v[l]), tm=ntok,
                       group=8).astype(BF16)
        x1 = _merge(alpha, xs, gout, sb, mo, w_gate, bg, w_bg, w_bs, w_bm, w_o, l1g, l1b)
        xs = _peer(alpha, x1, wq_t, sk, u_b, vt_b, l2g, l2b, tm=256)
        outs[4].append(k.reshape(nseq, ntok, SB_HEADS, SB_HD))
        outs[5].append(vv.reshape(nseq, ntok, SB_HEADS, SB_HD))
        outs[6].append(v.reshape(nseq, ntok, G_GROUPS, CHUNK))

    return (xp.reshape(batch, seq, d), xs.reshape(nseq, ntok, d)) + tuple(jnp.stack(o) for o in outs)
```

```python
import functools

import jax
import jax.numpy as jnp
from jax import lax
from jax.experimental import pallas as pl
from jax.experimental.pallas import tpu as pltpu

F32 = jnp.float32
BF16 = jnp.bfloat16

D_MODEL = 1024
G_WIDTH = 512
G_GROUPS = 4
CHUNK = 128
SB_HEADS = 8
SB_HD = 128
SB_WIDTH = 1024
MEM_HEADS = 4
MEM_HD = 128
MEM_WIDTH = 512
N_MEM = 256
N_BRANCH = 3
A_WIDTH = 2 * G_WIDTH + 3 * SB_WIDTH + MEM_WIDTH
PEER_HEADS = 8
PEER_DH = 128
N_KEYS = 128
PEER_TOPK = 16
PAGE = 128
LN_EPS = 1e-5
SB_SCALE = SB_HD ** -0.5
MEM_SCALE = MEM_HD ** -0.5

SB_CHUNK = 256
SB_BLOCK = 512
SB_HP = 4
PAGES_PER_STEP = 16
VMEM_LIMIT = 52 * 1024 * 1024


def _dot(a, b):
    return jnp.dot(a, b, preferred_element_type=F32)


def _dot_nt(a, b):
    return lax.dot_general(a, b, (((1,), (1,)), ((), ())), preferred_element_type=F32)


def _layer_norm(x, g, b):
    mu = jnp.mean(x, axis=-1, keepdims=True)
    xc = x - mu
    var = jnp.mean(xc * xc, axis=-1, keepdims=True)
    return xc * lax.rsqrt(var + LN_EPS) * g + b


def _params(*sem):
    return pltpu.CompilerParams(dimension_semantics=sem, vmem_limit_bytes=VMEM_LIMIT)


def _store_head_rows(ref, y, heads):
    rows, hd = y.shape[0], y.shape[1] // heads
    for h in range(heads):
        ref[pl.ds(h, rows, stride=heads), :] = y[:, h * hd:(h + 1) * hd]


def _load_head_rows(ref, h, heads):
    return ref[pl.ds(h, ref.shape[0] // heads, stride=heads), :]


def _proj_kernel(x_ref, w_ref, lng_ref, lnb_ref, ws_ref, bs_ref,
                 gout_ref, v_ref, qb_ref, k_ref, kb_ref, vv_ref, vvb_ref, qm_ref):
    tm = x_ref.shape[0]
    xb = x_ref[...].astype(BF16)
    u = jax.nn.gelu(_dot(xb, w_ref[:, 0:G_WIDTH]))
    gv = jax.nn.gelu(_dot(xb, w_ref[:, G_WIDTH:2 * G_WIDTH]))
    v = _layer_norm(gv, lng_ref[...], lnb_ref[...])
    v_ref[...] = v
    vb = v.astype(BF16)
    row = lax.broadcasted_iota(jnp.int32, (CHUNK, CHUNK), 0)
    col = lax.broadcasted_iota(jnp.int32, (CHUNK, CHUNK), 1)
    tril = col <= row
    for g in range(G_GROUPS):
        wg = jnp.where(tril, ws_ref[g], jnp.zeros((), BF16))
        cols = slice(g * CHUNK, (g + 1) * CHUNK)
        for c in range(tm // CHUNK):
            rows = slice(c * CHUNK, (c + 1) * CHUNK)
            s = _dot(wg, vb[rows, cols]) + bs_ref[:, cols]
            gout_ref[rows, cols] = (u[rows, cols] * s).astype(BF16)
    o = 2 * G_WIDTH
    qb_ref[...] = (_dot(xb, w_ref[:, o:o + SB_WIDTH]) * SB_SCALE).astype(BF16)
    k = _dot(xb, w_ref[:, o + SB_WIDTH:o + 2 * SB_WIDTH])
    _store_head_rows(k_ref, k, SB_HEADS)
    kb_ref[...] = k.astype(BF16)
    vv = _dot(xb, w_ref[:, o + 2 * SB_WIDTH:o + 3 * SB_WIDTH])
    _store_head_rows(vv_ref, vv, SB_HEADS)
    vvb_ref[...] = vv.astype(BF16)
    o = o + 3 * SB_WIDTH
    qm_ref[...] = (_dot(xb, w_ref[:, o:o + MEM_WIDTH]) * MEM_SCALE).astype(BF16)


def _proj(x2, w_a, lng, lnb, ws, bs_full, tm=256):
    n = x2.shape[0]
    row = lambda w: pl.BlockSpec((tm, w), lambda i: (i, 0))
    full = lambda a: pl.BlockSpec(a.shape, lambda i: (0,) * a.ndim)
    out_shape = (
        jax.ShapeDtypeStruct((n, G_WIDTH), BF16),
        jax.ShapeDtypeStruct((n, G_WIDTH), F32),
        jax.ShapeDtypeStruct((n, SB_WIDTH), BF16),
        jax.ShapeDtypeStruct((n * SB_HEADS, SB_HD), F32),
        jax.ShapeDtypeStruct((n, SB_WIDTH), BF16),
        jax.ShapeDtypeStruct((n * SB_HEADS, SB_HD), F32),
        jax.ShapeDtypeStruct((n, SB_WIDTH), BF16),
        jax.ShapeDtypeStruct((n, MEM_WIDTH), BF16),
    )
    return pl.pallas_call(
        _proj_kernel,
        grid=(n // tm,),
        in_specs=[row(D_MODEL), full(w_a), full(lng), full(lnb), full(ws), full(bs_full)],
        out_specs=tuple(pl.BlockSpec((tm * (s.shape[0] // n), s.shape[1]), lambda i: (i, 0)) for s in out_shape),
        out_shape=out_shape,
        compiler_params=_params("parallel"),
        name="proj",
    )(x2, w_a, lng, lnb, ws, bs_full)


def _mem_kv_kernel(x_ref, w_ref, k_ref, v_ref):
    y = _dot(x_ref[...].astype(BF16), w_ref[...])
    _store_head_rows(k_ref, y[:, :MEM_WIDTH], MEM_HEADS)
    _store_head_rows(v_ref, y[:, MEM_WIDTH:], MEM_HEADS)


def _mem_kv(mem2, w, tm=256):
    n = mem2.shape[0]
    out = jax.ShapeDtypeStruct((n * MEM_HEADS, MEM_HD), F32)
    spec = pl.BlockSpec((tm * MEM_HEADS, MEM_HD), lambda i: (i, 0))
    return pl.pallas_call(
        _mem_kv_kernel,
        grid=(n // tm,),
        in_specs=[pl.BlockSpec((tm, D_MODEL), lambda i: (i, 0)), pl.BlockSpec(w.shape, lambda i: (0, 0))],
        out_specs=(spec, spec),
        out_shape=(out, out),
        compiler_params=_params("parallel"),
        name="mem_kv",
    )(mem2, w)


def _sb_block(q, kb, vb, r_mat, bias, carry, causal):
    c = SB_CHUNK
    z = _dot_nt(q, kb) + bias
    sp = jnp.maximum(z, 0.0) + jnp.log(1.0 + jnp.exp(-jnp.abs(z)))
    if causal is not None:
        sp = jnp.where(causal, sp, 0.0)
    spb = sp.astype(BF16)
    nchunk = kb.shape[0] // c
    local = [_dot(spb[:, j * c:(j + 1) * c], r_mat) for j in range(nchunk)]
    pieces = [None] * nchunk
    for j in range(nchunk - 1, -1, -1):
        pieces[j] = jnp.exp(z[:, j * c:(j + 1) * c] - local[j] - carry)
        carry = carry + local[j][:, 0:1]
    a = pieces[0] if nchunk == 1 else jnp.concatenate(pieces, axis=1)
    if causal is not None:
        a = jnp.where(causal, a, 0.0)
    return _dot(a.astype(BF16), vb), carry


def _sb_prompt_kernel(bias_ref, q_ref, k_ref, v_ref, r_ref, o_ref):
    hg = pl.program_id(1)
    i = pl.program_id(2)
    t = SB_BLOCK
    r_mat = r_ref[...]
    row = lax.broadcasted_iota(jnp.int32, (t, t), 0)
    col = lax.broadcasted_iota(jnp.int32, (t, t), 1)
    heads = [(slice(j * SB_HD, (j + 1) * SB_HD), bias_ref[hg * SB_HP + j]) for j in range(SB_HP)]

    def visit(start, state, causal):
        out = []
        for (cols, bias), (acc, carry) in zip(heads, state):
            d, carry = _sb_block(q_ref[:, cols], k_ref[pl.ds(start, t), cols], v_ref[pl.ds(start, t), cols],
                                 r_mat, bias, carry, causal)
            out.append((acc + d, carry))
        return tuple(out)

    zero = (jnp.zeros((t, SB_HD), F32), jnp.zeros((t, 1), F32))
    state = visit(pl.multiple_of(i * t, t), (zero,) * SB_HP, col < row)
    def two_blocks(j, st):
        st = visit(pl.multiple_of((i - 1 - 2 * j) * t, t), st, None)
        return visit(pl.multiple_of((i - 2 - 2 * j) * t, t), st, None)

    state = lax.fori_loop(0, i // 2, two_blocks, state)
    state = lax.cond(i % 2 == 1, lambda st: visit(0, st, None), lambda st: st, state)
    for (cols, _), (acc, _) in zip(heads, state):
        o_ref[:, cols] = acc.astype(o_ref.dtype)


def _suffix_ones(n):
    j = lax.broadcasted_iota(jnp.int32, (n, n), 0)
    s = lax.broadcasted_iota(jnp.int32, (n, n), 1)
    return (j >= s).astype(BF16)


def _sb_prompt(qb, kb, vb, bias, batch, seq):
    t = SB_BLOCK
    nq = seq // t
    w = SB_HP * SB_HD
    return pl.pallas_call(
        _sb_prompt_kernel,
        grid_spec=pltpu.PrefetchScalarGridSpec(
            num_scalar_prefetch=1,
            grid=(batch, SB_HEADS // SB_HP, nq),
            in_specs=[
                pl.BlockSpec((t, w), lambda b, h, i, bias: (b * nq + i, h)),
                pl.BlockSpec((seq, w), lambda b, h, i, bias: (b, h)),
                pl.BlockSpec((seq, w), lambda b, h, i, bias: (b, h)),
                pl.BlockSpec((SB_CHUNK, SB_CHUNK), lambda b, h, i, bias: (0, 0)),
            ],
            out_specs=pl.BlockSpec((t, w), lambda b, h, i, bias: (b * nq + i, h)),
        ),
        out_shape=jax.ShapeDtypeStruct(qb.shape, BF16),
        compiler_params=_params("parallel", "parallel", "arbitrary"),
        name="sb_prompt",
    )(bias, qb, kb, vb, _suffix_ones(SB_CHUNK))


def _sb_sample_kernel(pt_ref, q_ref, kn_ref, vn_ref, *refs):
    npg = PAGES_PER_STEP
    k_refs = refs[:npg]
    v_refs = refs[npg:2 * npg]
    r_ref, bias_ref, o_ref, qbd_ref, acc_ref, carry_ref = refs[2 * npg:]
    s = pl.program_id(1)
    t = SB_CHUNK
    nrow = q_ref.shape[1] * SB_HEADS
    ntok = q_ref.shape[1]
    r_mat = r_ref[...]
    bias = bias_ref[...]

    def visit(kb, vb, causal):
        d, c = _sb_block(qbd_ref[...], kb, vb, r_mat, bias, carry_ref[:, 0:1], causal)
        acc_ref[...] += d
        carry_ref[...] = jnp.broadcast_to(c, carry_ref.shape)

    @pl.when(s == 0)
    def _():
        qrep = jnp.concatenate([q_ref[0]] * SB_HEADS, axis=0)
        rh = lax.broadcasted_iota(jnp.int32, qrep.shape, 0) // ntok
        ch = lax.broadcasted_iota(jnp.int32, qrep.shape, 1) // SB_HD
        qbd_ref[...] = jnp.where(rh == ch, qrep, 0.0).astype(BF16)
        acc_ref[...] = jnp.zeros_like(acc_ref)
        carry_ref[...] = jnp.zeros_like(carry_ref)
        tq = lax.broadcasted_iota(jnp.int32, (nrow, t), 0) % ntok
        tk = lax.broadcasted_iota(jnp.int32, (nrow, t), 1)
        pad = jnp.zeros((t - ntok, SB_WIDTH), F32)
        visit(jnp.concatenate([kn_ref[0], pad], axis=0).astype(BF16),
              jnp.concatenate([vn_ref[0], pad], axis=0).astype(BF16), tk < tq)

    def rows(page_ref):
        return jnp.concatenate([page_ref[pl.ds(h, PAGE, stride=SB_HEADS), :] for h in range(SB_HEADS)],
                               axis=1).astype(BF16)

    visit(jnp.concatenate([rows(r) for r in k_refs], axis=0), jnp.concatenate([rows(r) for r in v_refs], axis=0), None)

    @pl.when(s == pl.num_programs(1) - 1)
    def _():
        for h in range(SB_HEADS):
            cols = slice(h * SB_HD, (h + 1) * SB_HD)
            o_ref[0, :, cols] = acc_ref[h * ntok:(h + 1) * ntok, cols].astype(o_ref.dtype)


def _sb_sample(q3, kn3, vn3, cache_k, cache_v, layer, page_table, bias_rows):
    nseq, ntok, _ = q3.shape
    npages = page_table.shape[1]
    npg = PAGES_PER_STEP
    assert npages % npg == 0 and ntok <= SB_CHUNK
    nsteps = npages // npg
    nrow = ntok * SB_HEADS

    def page_spec(j):
        return pl.BlockSpec((None, None, PAGE * SB_HEADS, SB_HD),
                            lambda b, s, pt: (layer, pt[b, (nsteps - 1 - s) * npg + j], 0, 0))

    seq_spec = lambda r: pl.BlockSpec((1, r, SB_WIDTH), lambda b, s, pt: (b, 0, 0))
    return pl.pallas_call(
        _sb_sample_kernel,
        grid_spec=pltpu.PrefetchScalarGridSpec(
            num_scalar_prefetch=1,
            grid=(nseq, nsteps),
            in_specs=[seq_spec(ntok), seq_spec(ntok), seq_spec(ntok)]
            + [page_spec(j) for j in range(npg)] + [page_spec(j) for j in range(npg)]
            + [pl.BlockSpec((SB_CHUNK, SB_CHUNK), lambda b, s, pt: (0, 0)),
               pl.BlockSpec((nrow, 1), lambda b, s, pt: (0, 0))],
            out_specs=seq_spec(ntok),
            scratch_shapes=[pltpu.VMEM((nrow, SB_WIDTH), BF16),
                            pltpu.VMEM((nrow, SB_WIDTH), F32),
                            pltpu.VMEM((nrow, 128), F32)],
        ),
        out_shape=jax.ShapeDtypeStruct(q3.shape, F32),
        compiler_params=_params("parallel", "arbitrary"),
        name="sb_sample",
    )(page_table, q3, kn3, vn3, *([cache_k] * npg), *([cache_v] * npg), _suffix_ones(SB_CHUNK), bias_rows)


def _mem_attn_kernel(q_ref, k_ref, v_ref, o_ref):
    group = k_ref.shape[0]
    tm = q_ref.shape[0] // group
    for g in range(group):
        rows = slice(g * tm, (g + 1) * tm)
        for h in range(MEM_HEADS):
            cols = slice(h * MEM_HD, (h + 1) * MEM_HD)
            s = _dot_nt(q_ref[rows, cols].astype(BF16), _load_head_rows(k_ref.at[g], h, MEM_HEADS).astype(BF16))
            e = jnp.exp(s - jnp.max(s, axis=-1, keepdims=True))
            p = e / jnp.sum(e, axis=-1, keepdims=True)
            pv = _dot(p.astype(BF16), _load_head_rows(v_ref.at[g], h, MEM_HEADS).astype(BF16))
            o_ref[rows, cols] = pv.astype(o_ref.dtype)


def _mem_attn(qm, mk3, mv3, tm, group=1):
    n = qm.shape[0]
    nb = mk3.shape[0] // group
    nt = n // (nb * group * tm)
    assert mk3.shape[0] % group == 0 and (group == 1 or nt == 1)
    kv = pl.BlockSpec((group, N_MEM * MEM_HEADS, MEM_HD), lambda b, i: (b, 0, 0))
    qo = pl.BlockSpec((group * tm, MEM_WIDTH), lambda b, i: (b * nt + i, 0))
    return pl.pallas_call(
        _mem_attn_kernel,
        grid=(nb, nt),
        in_specs=[qo, kv, kv],
        out_specs=qo,
        out_shape=jax.ShapeDtypeStruct(qm.shape, qm.dtype),
        compiler_params=_params("parallel", "parallel"),
        name="mem_attn",
    )(qm, mk3, mv3)


def _merge_kernel(alpha, x_ref, g_ref, sb_ref, m_ref, wg_ref, bg_ref, wbg_ref, wbs_ref, wbm_ref, wo_ref,
                  lng_ref, lnb_ref, o_ref):
    x = x_ref[...]
    gates = jax.nn.sigmoid(_dot(x.astype(BF16), wg_ref[...]) + bg_ref[...])
    d = D_MODEL
    z = (gates[:, 0:d] * _dot(g_ref[...], wbg_ref[...])
         + gates[:, d:2 * d] * _dot(sb_ref[...], wbs_ref[...])
         + gates[:, 2 * d:3 * d] * _dot(m_ref[...], wbm_ref[...]))
    mix = _dot(z.astype(BF16), wo_ref[...])
    o_ref[...] = _layer_norm(alpha * x + mix, lng_ref[...], lnb_ref[...])


def _merge(alpha, x2, gout, sb, mo, w_gate, b_gate, w_bg, w_bs, w_bm, w_o, lng, lnb, tm=256):
    n = x2.shape[0]
    row = lambda a: pl.BlockSpec((tm, a.shape[1]), lambda i: (i, 0))
    full = lambda a: pl.BlockSpec(a.shape, lambda i: (0,) * a.ndim)
    ws = (w_gate, b_gate, w_bg, w_bs, w_bm, w_o, lng, lnb)
    return pl.pallas_call(
        functools.partial(_merge_kernel, alpha),
        grid=(n // tm,),
        in_specs=[row(x2), row(gout), row(sb), row(mo)] + [full(a) for a in ws],
        out_specs=row(x2),
        out_shape=jax.ShapeDtypeStruct(x2.shape, F32),
        compiler_params=_params("parallel"),
        name="merge",
    )(x2, gout, sb, mo, *ws)


def _cmpx(v, i, j):
    hi = jnp.maximum(v[i], v[j])
    lo = jnp.minimum(v[i], v[j])
    v[i], v[j] = hi, lo


def _sort16_desc(v):
    v = list(v)
    n = len(v)
    k = 2
    while k <= n:
        j = k // 2
        while j >= 1:
            for i in range(n):
                l = i ^ j
                if l > i:
                    if (i & k) == 0:
                        _cmpx(v, i, l)
                    else:
                        _cmpx(v, l, i)
            j //= 2
        k *= 2
    return v


def _merge_top16(a, b):
    n = len(a)
    c = []
    for i in range(n):
        j = n - 1 - i
        c.append(jnp.maximum(a[i], b[j]) if j < len(b) else a[i])
    j = n // 2
    while j >= 1:
        for i in range(n):
            if (i & j) == 0:
                _cmpx(c, i, i + j)
        j //= 2
    return c


def _top16_rows(s):
    g = _sort16_desc([s[8 * i:8 * i + 8, :] for i in range(N_KEYS // 8)])
    for shift in (4, 2, 1):
        g = _merge_top16(g, [pltpu.roll(x, shift, 0) for x in g])
    return g


def _tile_rows(a, reps):
    return jnp.concatenate([a] * reps, axis=0)


_PK_T1 = 0
_PK_T2 = PEER_TOPK
_PK_CNT = 2 * PEER_TOPK
_PK_INVZ = 3 * PEER_TOPK + 1
_PK_ROWS = 3 * PEER_TOPK + 2


def _peer_score_kernel(x_ref, wq_ref, sk_ref, xt_ref, nsel_ref, c_ref, rank_ref, p_ref, s1_s, s2_s, pk_ref):
    tm = x_ref.shape[0]
    xt = x_ref[...].T.astype(BF16)
    xt_ref[...] = xt
    k = PEER_TOPK
    sub = lax.broadcasted_iota(jnp.int32, (8, tm), 0)
    pk_ref[0:2 * k] = jnp.zeros((2 * k, 8, tm), F32)

    def scores(h, carry):
        qt = _dot(wq_ref[pl.ds(pl.multiple_of(h * 2 * PEER_DH, 2 * PEER_DH), 2 * PEER_DH), :], xt)
        s1 = _dot(sk_ref[2 * h], qt[:PEER_DH].astype(BF16))
        s2 = _dot(sk_ref[2 * h + 1], qt[PEER_DH:].astype(BF16))
        s1_s[h] = s1
        s2_s[h] = s2
        mine = sub == h
        for base, s in ((_PK_T1, s1), (_PK_T2, s2)):
            for a, t in enumerate(_top16_rows(s)):
                pk_ref[base + a] = jnp.where(mine, t, pk_ref[base + a])
        return carry

    lax.fori_loop(0, PEER_HEADS, scores, 0, unroll=4)

    t1 = [pk_ref[_PK_T1 + a] for a in range(k)]
    t2 = [pk_ref[_PK_T2 + a] for a in range(k)]
    top = [t1[0] + t2[b] for b in range(k)]
    for a in range(1, k):
        top = _merge_top16(top, [t1[a] + t2[b] for b in range(k // (a + 1))])
    tau = top[k - 1]
    m = t1[0] + t2[0]
    zsum = jnp.zeros_like(m)
    for a in range(k):
        cnt = jnp.zeros_like(m)
        for b in range(k // (a + 1)):
            c = t1[a] + t2[b]
            sel = c >= tau
            zsum = zsum + jnp.where(sel, jnp.exp(c - m), 0.0)
            cnt = cnt + jnp.where(sel, 1.0, 0.0)
        pk_ref[_PK_CNT + a] = cnt
    pk_ref[_PK_CNT + k] = jnp.zeros_like(m)
    pk_ref[_PK_INVZ] = 1.0 / zsum

    for h in range(PEER_HEADS):
        def row(i, cs):
            return jnp.broadcast_to(pk_ref[i, h:h + 1, cs], (N_KEYS, 128))

        for cc in range(tm // 128):
            cs = slice(cc * 128, (cc + 1) * 128)
            s2 = s2_s[h, :, cs]
            rank = jnp.zeros_like(s2)
            for a in range(k):
                rank = jnp.where(row(_PK_T2 + a, cs) > s2, float(a + 1), rank)
            rank_ref[h, :, cs] = rank.astype(BF16)
            p_ref[h, :, cs] = jnp.exp(s2 - row(_PK_T2, cs)).astype(BF16)
            s1 = s1_s[h, :, cs]
            nsel = row(_PK_CNT, cs)
            for a in range(k):
                nsel = jnp.where(row(_PK_T1 + a, cs) > s1, row(_PK_CNT + a + 1, cs), nsel)
            nsel_ref[h, :, cs] = nsel
            c_ref[h, :, cs] = jnp.exp(s1 - row(_PK_T1, cs)) * row(_PK_INVZ, cs)


def _peer_score(x1, wq_t, sk, tm):
    n = x1.shape[0]
    big = lambda dt: jax.ShapeDtypeStruct((PEER_HEADS, N_KEYS, n), dt)
    bspec = pl.BlockSpec((PEER_HEADS, N_KEYS, tm), lambda i: (0, 0, i))
    return pl.pallas_call(
        _peer_score_kernel,
        grid=(n // tm,),
        in_specs=[pl.BlockSpec((tm, D_MODEL), lambda i: (i, 0)),
                  pl.BlockSpec(wq_t.shape, lambda i: (0, 0)),
                  pl.BlockSpec(sk.shape, lambda i: (0, 0, 0))],
        out_specs=(pl.BlockSpec((D_MODEL, tm), lambda i: (0, i)), bspec, bspec, bspec, bspec),
        out_shape=(jax.ShapeDtypeStruct((D_MODEL, n), BF16), big(F32), big(F32), big(BF16), big(BF16)),
        scratch_shapes=[pltpu.VMEM((PEER_HEADS, N_KEYS, tm), F32),
                        pltpu.VMEM((PEER_HEADS, N_KEYS, tm), F32),
                        pltpu.VMEM((_PK_ROWS, 8, tm), F32)],
        compiler_params=_params("parallel"),
        name="peer_score",
    )(x1, wq_t, sk)


EXPERT_STEP = 2048
EXPERT_SUB = 256


def _peer_dense_kernel(alpha, x_ref, xt_ref, u_ref, vt_ref, nsel_ref, c_ref, rank_ref, p_ref,
                       lng_ref, lnb_ref, o_ref, acc_ref, act_ref, w_ref, rank_s, p_s):
    e = pl.program_id(1)
    tm = x_ref.shape[0]
    nhalf = EXPERT_SUB // N_KEYS
    nsub = EXPERT_STEP // EXPERT_SUB

    @pl.when(e == 0)
    def _():
        acc_ref[...] = jnp.zeros_like(acc_ref)
        rank_s[...] = rank_ref[...]
        p_s[...] = p_ref[...]

    def sel_row(ref, h, r, cs):
        tile = jnp.broadcast_to(ref[h, r:r + 1, cs], (16, 128)).astype(BF16)
        return jnp.concatenate([tile] * (N_KEYS // 16), axis=0)

    xt = xt_ref[...]
    for sub in range(nsub):
        rows = slice(sub * EXPERT_SUB, (sub + 1) * EXPERT_SUB)
        act_ref[rows, :] = jax.nn.gelu(_dot(u_ref[rows, :], xt).astype(BF16))
    for sub in range(nsub):
        for cc in range(tm // 128):
            cs = slice(cc * 128, (cc + 1) * 128)
            g = [jnp.zeros((N_KEYS, 128), BF16) for _ in range(nhalf)]
            for h in range(PEER_HEADS):
                rank = rank_s[h, :, cs]
                p = p_s[h, :, cs]
                for half in range(nhalf):
                    r = sub * nhalf + half
                    wsel = jnp.minimum(jnp.maximum(sel_row(nsel_ref, h, r, cs) - rank, 0), sel_row(c_ref, h, r, cs))
                    g[half] = g[half] + p * wsel
            for half in range(nhalf):
                er = slice(sub * EXPERT_SUB + half * N_KEYS, sub * EXPERT_SUB + (half + 1) * N_KEYS)
                w_ref[er, cs] = g[half] * act_ref[er, cs]
    acc_ref[...] += _dot(vt_ref[...], w_ref[...])

    @pl.when(e == pl.num_programs(1) - 1)
    def _():
        y = acc_ref[...].T
        o_ref[...] = _layer_norm(alpha * x_ref[...] + y, lng_ref[...], lnb_ref[...])


def _peer_dense(alpha, x1, xt, u_b, vt_b, nsel, c, rank, p, lng, lnb, tm):
    n = x1.shape[0]
    ne = u_b.shape[0] // EXPERT_STEP
    big = pl.BlockSpec((PEER_HEADS, N_KEYS, tm), lambda i, e: (0, 0, i))
    rowsel = pl.BlockSpec((PEER_HEADS, EXPERT_STEP // N_KEYS, tm), lambda i, e: (0, e, i))
    return pl.pallas_call(
        functools.partial(_peer_dense_kernel, alpha),
        grid=(n // tm, ne),
        in_specs=[pl.BlockSpec((tm, D_MODEL), lambda i, e: (i, 0)),
                  pl.BlockSpec((D_MODEL, tm), lambda i, e: (0, i)),
                  pl.BlockSpec((EXPERT_STEP, D_MODEL), lambda i, e: (e, 0)),
                  pl.BlockSpec((None, D_MODEL, EXPERT_STEP), lambda i, e: (e, 0, 0)),
                  rowsel, rowsel, big, big,
                  pl.BlockSpec(lng.shape, lambda i, e: (0, 0)),
                  pl.BlockSpec(lnb.shape, lambda i, e: (0, 0))],
        out_specs=pl.BlockSpec((tm, D_MODEL), lambda i, e: (i, 0)),
        out_shape=jax.ShapeDtypeStruct(x1.shape, F32),
        scratch_shapes=[pltpu.VMEM((D_MODEL, tm), F32),
                        pltpu.VMEM((EXPERT_STEP, tm), BF16),
                        pltpu.VMEM((EXPERT_STEP, tm), BF16),
                        pltpu.VMEM((PEER_HEADS, N_KEYS, tm), BF16),
                        pltpu.VMEM((PEER_HEADS, N_KEYS, tm), BF16)],
        compiler_params=_params("parallel", "arbitrary"),
        name="peer_dense",
    )(x1, xt, u_b, vt_b, nsel, c, rank, p, lng, lnb)


def _peer(alpha, x1, wq_t, sk, u_b, vt_b, lng, lnb, tm):
    xt, nsel, c, rank, p = _peer_score(x1, wq_t, sk, tm)
    return _peer_dense(alpha, x1, xt, u_b, vt_b, nsel, c, rank, p, lng, lnb, tm)


def kernel(x_prompt, x_sample, mem_prompt, cache_sb_k, cache_sb_v, page_table, cache_mem_k, cache_mem_v,
           w_in, b_gate, gmlp_ln_g, gmlp_ln_b, w_spatial, b_spatial, sb_bias, w_mem_kv, w_br_gmlp, w_br_sb,
           w_br_mem, w_out, ln1_g, ln1_b, peer_wq, peer_subkeys, peer_u, peer_v, ln2_g, ln2_b):
    depth = w_in.shape[0]
    alpha = float((2 * depth) ** 0.25)
    batch, seq, d = x_prompt.shape
    nseq, ntok, _ = x_sample.shape
    assert seq % SB_BLOCK == 0 and (nseq * ntok) % 256 == 0 and CHUNK % ntok == 0

    xp = x_prompt.reshape(batch * seq, d)
    xs = x_sample.reshape(nseq * ntok, d)
    outs = [[] for _ in range(7)]
    row2 = lambda a: a.reshape(1, -1)
    for l in range(depth):
        w_a = w_in[l][:, :A_WIDTH].astype(BF16)
        w_gate = w_in[l][:, A_WIDTH:].astype(BF16)
        bg = b_gate[l].reshape(1, N_BRANCH * D_MODEL)
        lng, lnb = row2(gmlp_ln_g[l]), row2(gmlp_ln_b[l])
        ws_p = w_spatial[l].astype(BF16)
        bs_p = jnp.repeat(b_spatial[l].T, CHUNK, axis=1)
        reps = CHUNK // ntok
        eye = jnp.eye(reps, dtype=F32)
        ws_s = jnp.einsum('ab,gij->gaibj', eye, w_spatial[l][:, :ntok, :ntok]).reshape(G_GROUPS, CHUNK, CHUNK).astype(BF16)
        bs_s = jnp.tile(bs_p[:ntok], (reps, 1))
        w_bg, w_bs, w_bm, w_o = (w.astype(BF16) for w in (w_br_gmlp[l], w_br_sb[l], w_br_mem[l], w_out[l]))
        l1g, l1b, l2g, l2b = row2(ln1_g[l]), row2(ln1_b[l]), row2(ln2_g[l]), row2(ln2_b[l])
        wq_t = peer_wq[l].T.astype(BF16)
        sk = peer_subkeys[l].reshape(PEER_HEADS * 2, N_KEYS, PEER_DH).astype(BF16)
        u_b = peer_u[l].astype(BF16)
        vt_b = peer_v[l].astype(BF16).reshape(-1, EXPERT_STEP, D_MODEL).transpose(0, 2, 1)
        bias = sb_bias[l].astype(F32)

        gout, _, qb, k, kb, vv, vvb, qm = _proj(xp, w_a, lng, lnb, ws_p, bs_p)
        mk, mv = _mem_kv(mem_prompt.reshape(-1, d), w_mem_kv[l].astype(BF16))
        mem_rows = lambda a: a.reshape(-1, N_MEM * MEM_HEADS, MEM_HD)
        mk3, mv3 = mem_rows(mk), mem_rows(mv)
        sb = _sb_prompt(qb, kb, vvb, bias, batch, seq)
        mo = _mem_attn(qm, mk3, mv3, tm=512)
        x1 = _merge(alpha, xp, gout, sb, mo, w_gate, bg, w_bg, w_bs, w_bm, w_o, l1g, l1b)
        xp = _peer(alpha, x1, wq_t, sk, u_b, vt_b, l2g, l2b, tm=512)
        outs[0].append(k.reshape(batch, seq, SB_HEADS, SB_HD))
        outs[1].append(vv.reshape(batch, seq, SB_HEADS, SB_HD))
        outs[2].append(mk.reshape(batch, N_MEM, MEM_HEADS, MEM_HD))
        outs[3].append(mv.reshape(batch, N_MEM, MEM_HEADS, MEM_HD))

        gout, v, qb, k, kb, vv, vvb, qm = _proj(xs, w_a, lng, lnb, ws_s, bs_s)
        bias_rows = jnp.repeat(bias, ntok).reshape(SB_HEADS * ntok, 1)
        seq_rows = lambda a: a.astype(F32).reshape(nseq, ntok, SB_WIDTH)
        pages = lambda c: c.reshape(c.shape[0], c.shape[1], PAGE * SB_HEADS, SB_HD)
        sb = _sb_sample(seq_rows(qb), seq_rows(kb), seq_rows(vvb), pages(cache_sb_k), pages(cache_sb_v), l,
                        page_table, bias_rows).reshape(nseq * ntok, SB_WIDTH).astype(BF16)
        mo = _mem_attn(qm.astype(F32), mem_rows(cache_mem_k[l]), mem_rows(cache_mem_v[l]), tm=ntok,
                       group=8).astype(BF16)
        x1 = _merge(alpha, xs, gout, sb, mo, w_gate, bg, w_bg, w_bs, w_bm, w_o, l1g, l1b)
        xs = _peer(alpha, x1, wq_t, sk, u_b, vt_b, l2g, l2b, tm=256)
        outs[4].append(k.reshape(nseq, ntok, SB_HEADS, SB_HD))
        outs[5].append(vv.reshape(nseq, ntok, SB_HEADS, SB_HD))
        outs[6].append(v.reshape(nseq, ntok, G_GROUPS, CHUNK))

    return (xp.reshape(batch, seq, d), xs.reshape(nseq, ntok, d)) + tuple(jnp.stack(o) for o in outs)
```

```python
import functools

import jax
import jax.numpy as jnp
from jax import lax
from jax.experimental import pallas as pl
from jax.experimental.pallas import tpu as pltpu

F32 = jnp.float32
BF16 = jnp.bfloat16

D_MODEL = 1024
G_WIDTH = 512
G_GROUPS = 4
CHUNK = 128
SB_HEADS = 8
SB_HD = 128
SB_WIDTH = 1024
MEM_HEADS = 4
MEM_HD = 128
MEM_WIDTH = 512
N_MEM = 256
N_BRANCH = 3
A_WIDTH = 2 * G_WIDTH + 3 * SB_WIDTH + MEM_WIDTH
PEER_HEADS = 8
PEER_DH = 128
N_KEYS = 128
PEER_TOPK = 16
PAGE = 128
LN_EPS = 1e-5
SB_SCALE = SB_HD ** -0.5
MEM_SCALE = MEM_HD ** -0.5

SB_CHUNK = 256
SB_BLOCK = 512
SB_HP = 4
PAGES_PER_STEP = 16
VMEM_LIMIT = 52 * 1024 * 1024


def _dot(a, b):
    return jnp.dot(a, b, preferred_element_type=F32)


def _dot_nt(a, b):
    return lax.dot_general(a, b, (((1,), (1,)), ((), ())), preferred_element_type=F32)


def _layer_norm(x, g, b):
    mu = jnp.mean(x, axis=-1, keepdims=True)
    xc = x - mu
    var = jnp.mean(xc * xc, axis=-1, keepdims=True)
    return xc * lax.rsqrt(var + LN_EPS) * g + b


def _params(*sem):
    return pltpu.CompilerParams(dimension_semantics=sem, vmem_limit_bytes=VMEM_LIMIT)


def _store_head_rows(ref, y, heads):
    rows, hd = y.shape[0], y.shape[1] // heads
    for h in range(heads):
        ref[pl.ds(h, rows, stride=heads), :] = y[:, h * hd:(h + 1) * hd]


def _load_head_rows(ref, h, heads):
    return ref[pl.ds(h, ref.shape[0] // heads, stride=heads), :]


def _proj_kernel(x_ref, w_ref, lng_ref, lnb_ref, ws_ref, bs_ref,
                 gout_ref, v_ref, qb_ref, k_ref, kb_ref, vv_ref, vvb_ref, qm_ref):
    tm = x_ref.shape[0]
    xb = x_ref[...].astype(BF16)
    u = jax.nn.gelu(_dot(xb, w_ref[:, 0:G_WIDTH]))
    gv = jax.nn.gelu(_dot(xb, w_ref[:, G_WIDTH:2 * G_WIDTH]))
    v = _layer_norm(gv, lng_ref[...], lnb_ref[...])
    v_ref[...] = v
    vb = v.astype(BF16)
    row = lax.broadcasted_iota(jnp.int32, (CHUNK, CHUNK), 0)
    col = lax.broadcasted_iota(jnp.int32, (CHUNK, CHUNK), 1)
    tril = col <= row
    for g in range(G_GROUPS):
        wg = jnp.where(tril, ws_ref[g], jnp.zeros((), BF16))
        cols = slice(g * CHUNK, (g + 1) * CHUNK)
        for c in range(tm // CHUNK):
            rows = slice(c * CHUNK, (c + 1) * CHUNK)
            s = _dot(wg, vb[rows, cols]) + bs_ref[:, cols]
            gout_ref[rows, cols] = (u[rows, cols] * s).astype(BF16)
    o = 2 * G_WIDTH
    qb_ref[...] = (_dot(xb, w_ref[:, o:o + SB_WIDTH]) * SB_SCALE).astype(BF16)
    k = _dot(xb, w_ref[:, o + SB_WIDTH:o + 2 * SB_WIDTH])
    _store_head_rows(k_ref, k, SB_HEADS)
    kb_ref[...] = k.astype(BF16)
    vv = _dot(xb, w_ref[:, o + 2 * SB_WIDTH:o + 3 * SB_WIDTH])
    _store_head_rows(vv_ref, vv, SB_HEADS)
    vvb_ref[...] = vv.astype(BF16)
    o = o + 3 * SB_WIDTH
    qm_ref[...] = (_dot(xb, w_ref[:, o:o + MEM_WIDTH]) * MEM_SCALE).astype(BF16)


def _proj(x2, w_a, lng, lnb, ws, bs_full, tm=256):
    n = x2.shape[0]
    row = lambda w: pl.BlockSpec((tm, w), lambda i: (i, 0))
    full = lambda a: pl.BlockSpec(a.shape, lambda i: (0,) * a.ndim)
    out_shape = (
        jax.ShapeDtypeStruct((n, G_WIDTH), BF16),
        jax.ShapeDtypeStruct((n, G_WIDTH), F32),
        jax.ShapeDtypeStruct((n, SB_WIDTH), BF16),
        jax.ShapeDtypeStruct((n * SB_HEADS, SB_HD), F32),
        jax.ShapeDtypeStruct((n, SB_WIDTH), BF16),
        jax.ShapeDtypeStruct((n * SB_HEADS, SB_HD), F32),
        jax.ShapeDtypeStruct((n, SB_WIDTH), BF16),
        jax.ShapeDtypeStruct((n, MEM_WIDTH), BF16),
    )
    return pl.pallas_call(
        _proj_kernel,
        grid=(n // tm,),
        in_specs=[row(D_MODEL), full(w_a), full(lng), full(lnb), full(ws), full(bs_full)],
        out_specs=tuple(pl.BlockSpec((tm * (s.shape[0] // n), s.shape[1]), lambda i: (i, 0)) for s in out_shape),
        out_shape=out_shape,
        compiler_params=_params("parallel"),
        name="proj",
    )(x2, w_a, lng, lnb, ws, bs_full)


def _mem_kv_kernel(x_ref, w_ref, k_ref, v_ref):
    y = _dot(x_ref[...].astype(BF16), w_ref[...])
    _store_head_rows(k_ref, y[:, :MEM_WIDTH], MEM_HEADS)
    _store_head_rows(v_ref, y[:, MEM_WIDTH:], MEM_HEADS)


def _mem_kv(mem2, w, tm=256):
    n = mem2.shape[0]
    out = jax.ShapeDtypeStruct((n * MEM_HEADS, MEM_HD), F32)
    spec = pl.BlockSpec((tm * MEM_HEADS, MEM_HD), lambda i: (i, 0))
    return pl.pallas_call(
        _mem_kv_kernel,
        grid=(n // tm,),
        in_specs=[pl.BlockSpec((tm, D_MODEL), lambda i: (i, 0)), pl.BlockSpec(w.shape, lambda i: (0, 0))],
        out_specs=(spec, spec),
        out_shape=(out, out),
        compiler_params=_params("parallel"),
        name="mem_kv",
    )(mem2, w)


def _sb_block(q, kb, vb, r_mat, bias, carry, causal):
    c = SB_CHUNK
    z = _dot_nt(q, kb) + bias
    sp = jnp.maximum(z, 0.0) + jnp.log(1.0 + jnp.exp(-jnp.abs(z)))
    if causal is not None:
        sp = jnp.where(causal, sp, 0.0)
    spb = sp.astype(BF16)
    nchunk = kb.shape[0] // c
    local = [_dot(spb[:, j * c:(j + 1) * c], r_mat) for j in range(nchunk)]
    pieces = [None] * nchunk
    for j in range(nchunk - 1, -1, -1):
        pieces[j] = jnp.exp(z[:, j * c:(j + 1) * c] - local[j] - carry)
        carry = carry + local[j][:, 0:1]
    a = pieces[0] if nchunk == 1 else jnp.concatenate(pieces, axis=1)
    if causal is not None:
        a = jnp.where(causal, a, 0.0)
    return _dot(a.astype(BF16), vb), carry


def _sb_prompt_kernel(bias_ref, q_ref, k_ref, v_ref, r_ref, o_ref):
    hg = pl.program_id(1)
    i = pl.program_id(2)
    t = SB_BLOCK
    r_mat = r_ref[...]
    row = lax.broadcasted_iota(jnp.int32, (t, t), 0)
    col = lax.broadcasted_iota(jnp.int32, (t, t), 1)
    heads = [(slice(j * SB_HD, (j + 1) * SB_HD), bias_ref[hg * SB_HP + j]) for j in range(SB_HP)]

    def visit(start, state, causal):
        out = []
        for (cols, bias), (acc, carry) in zip(heads, state):
            d, carry = _sb_block(q_ref[:, cols], k_ref[pl.ds(start, t), cols], v_ref[pl.ds(start, t), cols],
                                 r_mat, bias, carry, causal)
            out.append((acc + d, carry))
        return tuple(out)

    zero = (jnp.zeros((t, SB_HD), F32), jnp.zeros((t, 1), F32))
    state = visit(pl.multiple_of(i * t, t), (zero,) * SB_HP, col < row)
    def two_blocks(j, st):
        st = visit(pl.multiple_of((i - 1 - 2 * j) * t, t), st, None)
        return visit(pl.multiple_of((i - 2 - 2 * j) * t, t), st, None)

    state = lax.fori_loop(0, i // 2, two_blocks, state)
    state = lax.cond(i % 2 == 1, lambda st: visit(0, st, None), lambda st: st, state)
    for (cols, _), (acc, _) in zip(heads, state):
        o_ref[:, cols] = acc.astype(o_ref.dtype)


def _suffix_ones(n):
    j = lax.broadcasted_iota(jnp.int32, (n, n), 0)
    s = lax.broadcasted_iota(jnp.int32, (n, n), 1)
    return (j >= s).astype(BF16)


def _sb_prompt(qb, kb, vb, bias, batch, seq):
    t = SB_BLOCK
    nq = seq // t
    w = SB_HP * SB_HD
    return pl.pallas_call(
        _sb_prompt_kernel,
        grid_spec=pltpu.PrefetchScalarGridSpec(
            num_scalar_prefetch=1,
            grid=(batch, SB_HEADS // SB_HP, nq),
            in_specs=[
                pl.BlockSpec((t, w), lambda b, h, i, bias: (b * nq + i, h)),
                pl.BlockSpec((seq, w), lambda b, h, i, bias: (b, h)),
                pl.BlockSpec((seq, w), lambda b, h, i, bias: (b, h)),
                pl.BlockSpec((SB_CHUNK, SB_CHUNK), lambda b, h, i, bias: (0, 0)),
            ],
            out_specs=pl.BlockSpec((t, w), lambda b, h, i, bias: (b * nq + i, h)),
        ),
        out_shape=jax.ShapeDtypeStruct(qb.shape, BF16),
        compiler_params=_params("parallel", "parallel", "arbitrary"),
        name="sb_prompt",
    )(bias, qb, kb, vb, _suffix_ones(SB_CHUNK))


def _sb_sample_kernel(pt_ref, q_ref, kn_ref, vn_ref, *refs):
    npg = PAGES_PER_STEP
    k_refs = refs[:npg]
    v_refs = refs[npg:2 * npg]
    r_ref, bias_ref, o_ref, qbd_ref, acc_ref, carry_ref = refs[2 * npg:]
    s = pl.program_id(1)
    t = SB_CHUNK
    nrow = q_ref.shape[1] * SB_HEADS
    ntok = q_ref.shape[1]
    r_mat = r_ref[...]
    bias = bias_ref[...]

    def visit(kb, vb, causal):
        d, c = _sb_block(qbd_ref[...], kb, vb, r_mat, bias, carry_ref[:, 0:1], causal)
        acc_ref[...] += d
        carry_ref[...] = jnp.broadcast_to(c, carry_ref.shape)

    @pl.when(s == 0)
    def _():
        qrep = jnp.concatenate([q_ref[0]] * SB_HEADS, axis=0)
        rh = lax.broadcasted_iota(jnp.int32, qrep.shape, 0) // ntok
        ch = lax.broadcasted_iota(jnp.int32, qrep.shape, 1) // SB_HD
        qbd_ref[...] = jnp.where(rh == ch, qrep, 0.0).astype(BF16)
        acc_ref[...] = jnp.zeros_like(acc_ref)
        carry_ref[...] = jnp.zeros_like(carry_ref)
        tq = lax.broadcasted_iota(jnp.int32, (nrow, t), 0) % ntok
        tk = lax.broadcasted_iota(jnp.int32, (nrow, t), 1)
        pad = jnp.zeros((t - ntok, SB_WIDTH), F32)
        visit(jnp.concatenate([kn_ref[0], pad], axis=0).astype(BF16),
              jnp.concatenate([vn_ref[0], pad], axis=0).astype(BF16), tk < tq)

    def rows(page_ref):
        return jnp.concatenate([page_ref[pl.ds(h, PAGE, stride=SB_HEADS), :] for h in range(SB_HEADS)],
                               axis=1).astype(BF16)

    visit(jnp.concatenate([rows(r) for r in k_refs], axis=0), jnp.concatenate([rows(r) for r in v_refs], axis=0), None)

    @pl.when(s == pl.num_programs(1) - 1)
    def _():
        for h in range(SB_HEADS):
            cols = slice(h * SB_HD, (h + 1) * SB_HD)
            o_ref[0, :, cols] = acc_ref[h * ntok:(h + 1) * ntok, cols].astype(o_ref.dtype)


def _sb_sample(q3, kn3, vn3, cache_k, cache_v, layer, page_table, bias_rows):
    nseq, ntok, _ = q3.shape
    npages = page_table.shape[1]
    npg = PAGES_PER_STEP
    assert npages % npg == 0 and ntok <= SB_CHUNK
    nsteps = npages // npg
    nrow = ntok * SB_HEADS

    def page_spec(j):
        return pl.BlockSpec((None, None, PAGE * SB_HEADS, SB_HD),
                            lambda b, s, pt: (layer, pt[b, (nsteps - 1 - s) * npg + j], 0, 0))

    seq_spec = lambda r: pl.BlockSpec((1, r, SB_WIDTH), lambda b, s, pt: (b, 0, 0))
    return pl.pallas_call(
        _sb_sample_kernel,
        grid_spec=pltpu.PrefetchScalarGridSpec(
            num_scalar_prefetch=1,
            grid=(nseq, nsteps),
            in_specs=[seq_spec(ntok), seq_spec(ntok), seq_spec(ntok)]
            + [page_spec(j) for j in range(npg)] + [page_spec(j) for j in range(npg)]
            + [pl.BlockSpec((SB_CHUNK, SB_CHUNK), lambda b, s, pt: (0, 0)),
               pl.BlockSpec((nrow, 1), lambda b, s, pt: (0, 0))],
            out_specs=seq_spec(ntok),
            scratch_shapes=[pltpu.VMEM((nrow, SB_WIDTH), BF16),
                            pltpu.VMEM((nrow, SB_WIDTH), F32),
                            pltpu.VMEM((nrow, 128), F32)],
        ),
        out_shape=jax.ShapeDtypeStruct(q3.shape, F32),
        compiler_params=_params("parallel", "arbitrary"),
        name="sb_sample",
    )(page_table, q3, kn3, vn3, *([cache_k] * npg), *([cache_v] * npg), _suffix_ones(SB_CHUNK), bias_rows)


def _mem_attn_kernel(q_ref, k_ref, v_ref, o_ref):
    group = k_ref.shape[0]
    tm = q_ref.shape[0] // group
    for g in range(group):
        rows = slice(g * tm, (g + 1) * tm)
        for h in range(MEM_HEADS):
            cols = slice(h * MEM_HD, (h + 1) * MEM_HD)
            s = _dot_nt(q_ref[rows, cols].astype(BF16), _load_head_rows(k_ref.at[g], h, MEM_HEADS).astype(BF16))
            e = jnp.exp(s - jnp.max(s, axis=-1, keepdims=True))
            p = e / jnp.sum(e, axis=-1, keepdims=True)
            pv = _dot(p.astype(BF16), _load_head_rows(v_ref.at[g], h, MEM_HEADS).astype(BF16))
            o_ref[rows, cols] = pv.astype(o_ref.dtype)


def _mem_attn(qm, mk3, mv3, tm, group=1):
    n = qm.shape[0]
    nb = mk3.shape[0] // group
    nt = n // (nb * group * tm)
    assert mk3.shape[0] % group == 0 and (group == 1 or nt == 1)
    kv = pl.BlockSpec((group, N_MEM * MEM_HEADS, MEM_HD), lambda b, i: (b, 0, 0))
    qo = pl.BlockSpec((group * tm, MEM_WIDTH), lambda b, i: (b * nt + i, 0))
    return pl.pallas_call(
        _mem_attn_kernel,
        grid=(nb, nt),
        in_specs=[qo, kv, kv],
        out_specs=qo,
        out_shape=jax.ShapeDtypeStruct(qm.shape, qm.dtype),
        compiler_params=_params("parallel", "parallel"),
        name="mem_attn",
    )(qm, mk3, mv3)


def _merge_kernel(alpha, x_ref, g_ref, sb_ref, m_ref, wg_ref, bg_ref, wbg_ref, wbs_ref, wbm_ref, wo_ref,
                  lng_ref, lnb_ref, o_ref):
    x = x_ref[...]
    gates = jax.nn.sigmoid(_dot(x.astype(BF16), wg_ref[...]) + bg_ref[...])
    d = D_MODEL
    z = (gates[:, 0:d] * _dot(g_ref[...], wbg_ref[...])
         + gates[:, d:2 * d] * _dot(sb_ref[...], wbs_ref[...])
         + gates[:, 2 * d:3 * d] * _dot(m_ref[...], wbm_ref[...]))
    mix = _dot(z.astype(BF16), wo_ref[...])
    o_ref[...] = _layer_norm(alpha * x + mix, lng_ref[...], lnb_ref[...])


def _merge(alpha, x2, gout, sb, mo, w_gate, b_gate, w_bg, w_bs, w_bm, w_o, lng, lnb, tm=256):
    n = x2.shape[0]
    row = lambda a: pl.BlockSpec((tm, a.shape[1]), lambda i: (i, 0))
    full = lambda a: pl.BlockSpec(a.shape, lambda i: (0,) * a.ndim)
    ws = (w_gate, b_gate, w_bg, w_bs, w_bm, w_o, lng, lnb)
    return pl.pallas_call(
        functools.partial(_merge_kernel, alpha),
        grid=(n // tm,),
        in_specs=[row(x2), row(gout), row(sb), row(mo)] + [full(a) for a in ws],
        out_specs=row(x2),
        out_shape=jax.ShapeDtypeStruct(x2.shape, F32),
        compiler_params=_params("parallel"),
        name="merge",
    )(x2, gout, sb, mo, *ws)


def _cmpx(v, i, j):
    hi = jnp.maximum(v[i], v[j])
    lo = jnp.minimum(v[i], v[j])
    v[i], v[j] = hi, lo


def _sort16_desc(v):
    v = list(v)
    n = len(v)
    k = 2
    while k <= n:
        j = k // 2
        while j >= 1:
            for i in range(n):
                l = i ^ j
                if l > i:
                    if (i & k) == 0:
                        _cmpx(v, i, l)
                    else:
                        _cmpx(v, l, i)
            j //= 2
        k *= 2
    return v


def _merge_top16(a, b):
    n = len(a)
    c = []
    for i in range(n):
        j = n - 1 - i
        c.append(jnp.maximum(a[i], b[j]) if j < len(b) else a[i])
    j = n // 2
    while j >= 1:
        for i in range(n):
            if (i & j) == 0:
                _cmpx(c, i, i + j)
        j //= 2
    return c


def _top16_rows(s):
    g = _sort16_desc([s[8 * i:8 * i + 8, :] for i in range(N_KEYS // 8)])
    for shift in (4, 2, 1):
        g = _merge_top16(g, [pltpu.roll(x, shift, 0) for x in g])
    return g


def _tile_rows(a, reps):
    return jnp.concatenate([a] * reps, axis=0)


_PK_T1 = 0
_PK_T2 = PEER_TOPK
_PK_CNT = 2 * PEER_TOPK
_PK_INVZ = 3 * PEER_TOPK + 1
_PK_ROWS = 3 * PEER_TOPK + 2


def _peer_score_kernel(x_ref, wq_ref, sk_ref, xt_ref, nsel_ref, c_ref, rank_ref, p_ref, s1_s, s2_s, pk_ref):
    tm = x_ref.shape[0]
    xt = x_ref[...].T.astype(BF16)
    xt_ref[...] = xt
    k = PEER_TOPK
    sub = lax.broadcasted_iota(jnp.int32, (8, tm), 0)
    pk_ref[0:2 * k] = jnp.zeros((2 * k, 8, tm), F32)

    def scores(h, carry):
        qt = _dot(wq_ref[pl.ds(pl.multiple_of(h * 2 * PEER_DH, 2 * PEER_DH), 2 * PEER_DH), :], xt)
        s1 = _dot(sk_ref[2 * h], qt[:PEER_DH].astype(BF16))
        s2 = _dot(sk_ref[2 * h + 1], qt[PEER_DH:].astype(BF16))
        s1_s[h] = s1
        s2_s[h] = s2
        mine = sub == h
        for base, s in ((_PK_T1, s1), (_PK_T2, s2)):
            for a, t in enumerate(_top16_rows(s)):
                pk_ref[base + a] = jnp.where(mine, t, pk_ref[base + a])
        return carry

    lax.fori_loop(0, PEER_HEADS, scores, 0, unroll=4)

    t1 = [pk_ref[_PK_T1 + a] for a in range(k)]
    t2 = [pk_ref[_PK_T2 + a] for a in range(k)]
    top = [t1[0] + t2[b] for b in range(k)]
    for a in range(1, k):
        top = _merge_top16(top, [t1[a] + t2[b] for b in range(k // (a + 1))])
    tau = top[k - 1]
    m = t1[0] + t2[0]
    zsum = jnp.zeros_like(m)
    for a in range(k):
        cnt = jnp.zeros_like(m)
        for b in range(k // (a + 1)):
            c = t1[a] + t2[b]
            sel = c >= tau
            zsum = zsum + jnp.where(sel, jnp.exp(c - m), 0.0)
            cnt = cnt + jnp.where(sel, 1.0, 0.0)
        pk_ref[_PK_CNT + a] = cnt
    pk_ref[_PK_CNT + k] = jnp.zeros_like(m)
    pk_ref[_PK_INVZ] = 1.0 / zsum

    for h in range(PEER_HEADS):
        def row(i, cs):
            return jnp.broadcast_to(pk_ref[i, h:h + 1, cs], (N_KEYS, 128))

        for cc in range(tm // 128):
            cs = slice(cc * 128, (cc + 1) * 128)
            s2 = s2_s[h, :, cs]
            rank = jnp.zeros_like(s2)
            for a in range(k):
                rank = jnp.where(row(_PK_T2 + a, cs) > s2, float(a + 1), rank)
            rank_ref[h, :, cs] = rank.astype(BF16)
            p_ref[h, :, cs] = jnp.exp(s2 - row(_PK_T2, cs)).astype(BF16)
            s1 = s1_s[h, :, cs]
            nsel = row(_PK_CNT, cs)
            for a in range(k):
                nsel = jnp.where(row(_PK_T1 + a, cs) > s1, row(_PK_CNT + a + 1, cs), nsel)
            nsel_ref[h, :, cs] = nsel
            c_ref[h, :, cs] = jnp.exp(s1 - row(_PK_T1, cs)) * row(_PK_INVZ, cs)


def _peer_score(x1, wq_t, sk, tm):
    n = x1.shape[0]
    big = lambda dt: jax.ShapeDtypeStruct((PEER_HEADS, N_KEYS, n), dt)
    bspec = pl.BlockSpec((PEER_HEADS, N_KEYS, tm), lambda i: (0, 0, i))
    return pl.pallas_call(
        _peer_score_kernel,
        grid=(n // tm,),
        in_specs=[pl.BlockSpec((tm, D_MODEL), lambda i: (i, 0)),
                  pl.BlockSpec(wq_t.shape, lambda i: (0, 0)),
                  pl.BlockSpec(sk.shape, lambda i: (0, 0, 0))],
        out_specs=(pl.BlockSpec((D_MODEL, tm), lambda i: (0, i)), bspec, bspec, bspec, bspec),
        out_shape=(jax.ShapeDtypeStruct((D_MODEL, n), BF16), big(F32), big(F32), big(BF16), big(BF16)),
        scratch_shapes=[pltpu.VMEM((PEER_HEADS, N_KEYS, tm), F32),
                        pltpu.VMEM((PEER_HEADS, N_KEYS, tm), F32),
                        pltpu.VMEM((_PK_ROWS, 8, tm), F32)],
        compiler_params=_params("parallel"),
        name="peer_score",
    )(x1, wq_t, sk)


EXPERT_STEP = 2048
EXPERT_SUB = 256


def _peer_dense_kernel(alpha, x_ref, xt_ref, u_ref, vt_ref, nsel_ref, c_ref, rank_ref, p_ref,
                       lng_ref, lnb_ref, o_ref, acc_ref, act_ref, w_ref, rank_s, p_s):
    e = pl.program_id(1)
    tm = x_ref.shape[0]
    nhalf = EXPERT_SUB // N_KEYS
    nsub = EXPERT_STEP // EXPERT_SUB

    @pl.when(e == 0)
    def _():
        acc_ref[...] = jnp.zeros_like(acc_ref)
        rank_s[...] = rank_ref[...]
        p_s[...] = p_ref[...]

    def sel_row(ref, h, r, cs):
        tile = jnp.broadcast_to(ref[h, r:r + 1, cs], (16, 128)).astype(BF16)
        return jnp.concatenate([tile] * (N_KEYS // 16), axis=0)

    xt = xt_ref[...]
    for sub in range(nsub):
        rows = slice(sub * EXPERT_SUB, (sub + 1) * EXPERT_SUB)
        act_ref[rows, :] = jax.nn.gelu(_dot(u_ref[rows, :], xt).astype(BF16))
    for sub in range(nsub):
        for cc in range(tm // 128):
            cs = slice(cc * 128, (cc + 1) * 128)
            g = [jnp.zeros((N_KEYS, 128), BF16) for _ in range(nhalf)]
            for h in range(PEER_HEADS):
                rank = rank_s[h, :, cs]
                p = p_s[h, :, cs]
                for half in range(nhalf):
                    r = sub * nhalf + half
                    wsel = jnp.minimum(jnp.maximum(sel_row(nsel_ref, h, r, cs) - rank, 0), sel_row(c_ref, h, r, cs))
                    g[half] = g[half] + p * wsel
            for half in range(nhalf):
                er = slice(sub * EXPERT_SUB + half * N_KEYS, sub * EXPERT_SUB + (half + 1) * N_KEYS)
                w_ref[er, cs] = g[half] * act_ref[er, cs]
    acc_ref[...] += lax.dot_general(w_ref[...], vt_ref[...], (((0,), (0,)), ((), ())), preferred_element_type=F32)

    @pl.when(e == pl.num_programs(1) - 1)
    def _():
        y = acc_ref[...]
        o_ref[...] = _layer_norm(alpha * x_ref[...] + y, lng_ref[...], lnb_ref[...])


def _peer_dense(alpha, x1, xt, u_b, vt_b, nsel, c, rank, p, lng, lnb, tm):
    n = x1.shape[0]
    ne = u_b.shape[0] // EXPERT_STEP
    big = pl.BlockSpec((PEER_HEADS, N_KEYS, tm), lambda i, e: (0, 0, i))
    rowsel = pl.BlockSpec((PEER_HEADS, EXPERT_STEP // N_KEYS, tm), lambda i, e: (0, e, i))
    return pl.pallas_call(
        functools.partial(_peer_dense_kernel, alpha),
        grid=(n // tm, ne),
        in_specs=[pl.BlockSpec((tm, D_MODEL), lambda i, e: (i, 0)),
                  pl.BlockSpec((D_MODEL, tm), lambda i, e: (0, i)),
                  pl.BlockSpec((EXPERT_STEP, D_MODEL), lambda i, e: (e, 0)),
                  pl.BlockSpec((None, EXPERT_STEP, D_MODEL), lambda i, e: (e, 0, 0)),
                  rowsel, rowsel, big, big,
                  pl.BlockSpec(lng.shape, lambda i, e: (0, 0)),
                  pl.BlockSpec(lnb.shape, lambda i, e: (0, 0))],
        out_specs=pl.BlockSpec((tm, D_MODEL), lambda i, e: (i, 0)),
        out_shape=jax.ShapeDtypeStruct(x1.shape, F32),
        scratch_shapes=[pltpu.VMEM((tm, D_MODEL), F32),
                        pltpu.VMEM((EXPERT_STEP, tm), BF16),
                        pltpu.VMEM((EXPERT_STEP, tm), BF16),
                        pltpu.VMEM((PEER_HEADS, N_KEYS, tm), BF16),
                        pltpu.VMEM((PEER_HEADS, N_KEYS, tm), BF16)],
        compiler_params=_params("parallel", "arbitrary"),
        name="peer_dense",
    )(x1, xt, u_b, vt_b, nsel, c, rank, p, lng, lnb)


def _peer(alpha, x1, wq_t, sk, u_b, vt_b, lng, lnb, tm):
    xt, nsel, c, rank, p = _peer_score(x1, wq_t, sk, tm)
    return _peer_dense(alpha, x1, xt, u_b, vt_b, nsel, c, rank, p, lng, lnb, tm)


def kernel(x_prompt, x_sample, mem_prompt, cache_sb_k, cache_sb_v, page_table, cache_mem_k, cache_mem_v,
           w_in, b_gate, gmlp_ln_g, gmlp_ln_b, w_spatial, b_spatial, sb_bias, w_mem_kv, w_br_gmlp, w_br_sb,
           w_br_mem, w_out, ln1_g, ln1_b, peer_wq, peer_subkeys, peer_u, peer_v, ln2_g, ln2_b):
    depth = w_in.shape[0]
    alpha = float((2 * depth) ** 0.25)
    batch, seq, d = x_prompt.shape
    nseq, ntok, _ = x_sample.shape
    assert seq % SB_BLOCK == 0 and (nseq * ntok) % 256 == 0 and CHUNK % ntok == 0

    xp = x_prompt.reshape(batch * seq, d)
    xs = x_sample.reshape(nseq * ntok, d)
    outs = [[] for _ in range(7)]
    row2 = lambda a: a.reshape(1, -1)
    for l in range(depth):
        w_a = w_in[l][:, :A_WIDTH].astype(BF16)
        w_gate = w_in[l][:, A_WIDTH:].astype(BF16)
        bg = b_gate[l].reshape(1, N_BRANCH * D_MODEL)
        lng, lnb = row2(gmlp_ln_g[l]), row2(gmlp_ln_b[l])
        ws_p = w_spatial[l].astype(BF16)
        bs_p = jnp.repeat(b_spatial[l].T, CHUNK, axis=1)
        reps = CHUNK // ntok
        eye = jnp.eye(reps, dtype=F32)
        ws_s = jnp.einsum('ab,gij->gaibj', eye, w_spatial[l][:, :ntok, :ntok]).reshape(G_GROUPS, CHUNK, CHUNK).astype(BF16)
        bs_s = jnp.tile(bs_p[:ntok], (reps, 1))
        w_bg, w_bs, w_bm, w_o = (w.astype(BF16) for w in (w_br_gmlp[l], w_br_sb[l], w_br_mem[l], w_out[l]))
        l1g, l1b, l2g, l2b = row2(ln1_g[l]), row2(ln1_b[l]), row2(ln2_g[l]), row2(ln2_b[l])
        wq_t = peer_wq[l].T.astype(BF16)
        sk = peer_subkeys[l].reshape(PEER_HEADS * 2, N_KEYS, PEER_DH).astype(BF16)
        u_b = peer_u[l].astype(BF16)
        vt_b = peer_v[l].astype(BF16).reshape(-1, EXPERT_STEP, D_MODEL)
        bias = sb_bias[l].astype(F32)

        gout, _, qb, k, kb, vv, vvb, qm = _proj(xp, w_a, lng, lnb, ws_p, bs_p)
        mk, mv = _mem_kv(mem_prompt.reshape(-1, d), w_mem_kv[l].astype(BF16))
        mem_rows = lambda a: a.reshape(-1, N_MEM * MEM_HEADS, MEM_HD)
        mk3, mv3 = mem_rows(mk), mem_rows(mv)
        sb = _sb_prompt(qb, kb, vvb, bias, batch, seq)
        mo = _mem_attn(qm, mk3, mv3, tm=512)
        x1 = _merge(alpha, xp, gout, sb, mo, w_gate, bg, w_bg, w_bs, w_bm, w_o, l1g, l1b)
        xp = _peer(alpha, x1, wq_t, sk, u_b, vt_b, l2g, l2b, tm=512)
        outs[0].append(k.reshape(batch, seq, SB_HEADS, SB_HD))
        outs[1].append(vv.reshape(batch, seq, SB_HEADS, SB_HD))
        outs[2].append(mk.reshape(batch, N_MEM, MEM_HEADS, MEM_HD))
        outs[3].append(mv.reshape(batch, N_MEM, MEM_HEADS, MEM_HD))

        gout, v, qb, k, kb, vv, vvb, qm = _proj(xs, w_a, lng, lnb, ws_s, bs_s)
        bias_rows = jnp.repeat(bias, ntok).reshape(SB_HEADS * ntok, 1)
        seq_rows = lambda a: a.astype(F32).reshape(nseq, ntok, SB_WIDTH)
        pages = lambda c: c.reshape(c.shape[0], c.shape[1], PAGE * SB_HEADS, SB_HD)
        sb = _sb_sample(seq_rows(qb), seq_rows(kb), seq_rows(vvb), pages(cache_sb_k), pages(cache_sb_v), l,
                        page_table, bias_rows).reshape(nseq * ntok, SB_WIDTH).astype(BF16)
        mo = _mem_attn(qm.astype(F32), mem_rows(cache_mem_k[l]), mem_rows(cache_mem_v[l]), tm=ntok,
                       group=8).astype(BF16)
        x1 = _merge(alpha, xs, gout, sb, mo, w_gate, bg, w_bg, w_bs, w_bm, w_o, l1g, l1b)
        xs = _peer(alpha, x1, wq_t, sk, u_b, vt_b, l2g, l2b, tm=256)
        outs[4].append(k.reshape(nseq, ntok, SB_HEADS, SB_HD))
        outs[5].append(vv.reshape(nseq, ntok, SB_HEADS, SB_HD))
        outs[6].append(v.reshape(nseq, ntok, G_GROUPS, CHUNK))

    return (xp.reshape(batch, seq, d), xs.reshape(nseq, ntok, d)) + tuple(jnp.stack(o) for o in outs)
```
